```python
import math
import jax, jax.numpy as jnp
from jax import lax
import numpy as np

D_MODEL = 1024
BATCH = 2
SEQ = 8192
DEPTH = 4
DEC_BATCH = 8
DEC_SEQ = 32
PAST_LEN = 1024

CHUNK = 64
D_MIX = D_MODEL
HEAD_DIM = 64
D_A = D_MIX // 4
N_BLK_A = D_A // HEAD_DIM
CONV_W = 4
LRU_C = 8.0
D_B = 3 * D_MIX // 8
N_HEAD_B = D_B // HEAD_DIM
LORA_W = 64
LORA_A = 64
LORA_G = 128
D_B_IN = 3 * D_B + LORA_W + LORA_A + LORA_G
D_C = D_MIX - D_A - D_B
N_HEAD_C = D_C // HEAD_DIM
D_IN = 2 * D_A + D_B_IN + 3 * D_C
D_FF = -(-8 * D_MODEL // (3 * 256)) * 256
RMS_EPS = 1e-6
GN_EPS_B = 64e-5
GN_EPS_C = 1e-6

kernel_name = 'hybrid_rglru_rwkv7_mlstm_stream_step'


def rmsnorm(x, g, eps=RMS_EPS):
    xf = x.astype(jnp.float32)
    y = xf * lax.rsqrt(jnp.mean(xf * xf, axis=-1, keepdims=True) + eps)
    return (y * g.astype(jnp.float32)).astype(x.dtype)


def head_norm(y, eps):
    yc = y - jnp.mean(y, axis=-1, keepdims=True)
    return yc * lax.rsqrt(jnp.mean(yc * yc, axis=-1, keepdims=True) + eps)


def causal_conv(x, buf, w, b):
    T = x.shape[1]
    xp = jnp.concatenate([buf, x], axis=1)
    y = b
    for j in range(CONV_W):
        y = y + w[j] * xp[:, j:j + T]
    return y, xp[:, -(CONV_W - 1):]


def token_shift(x, buf):
    xp = jnp.concatenate([buf, x], axis=1)
    return xp[:, :-1], xp[:, -1:]


def rg_lru(x, h0, w_r, b_r, w_i, b_i, lam):
    Bn, T, _ = x.shape
    xb = x.reshape(Bn, T, N_BLK_A, HEAD_DIM)
    gate_r = jax.nn.sigmoid(jnp.einsum('btnd,nde->btne', xb, w_r).reshape(Bn, T, D_A) + b_r)
    gate_i = jax.nn.sigmoid(jnp.einsum('btnd,nde->btne', xb, w_i).reshape(Bn, T, D_A) + b_i)
    log_a = -LRU_C * gate_r * jax.nn.softplus(-lam)
    a = jnp.exp(log_a)
    u = jnp.sqrt(-jnp.expm1(2.0 * log_a)) * (gate_i * x)

    def combine(left, right):
        a1, b1 = left
        a2, b2 = right
        return a1 * a2, a2 * b1 + b2

    a_cum, h = lax.associative_scan(combine, (a, u), axis=1)
    h = h + a_cum * h0[:, None, :]
    return h, h[:, -1]


def rwkv7_mix(pb, shift_buf, S0, mu, w0, w2, a0, a2, g2, k_k, k_a, r_k, ln_w, ln_b):
    Bn, T, _ = pb.shape
    prev, shift_new = token_shift(pb, shift_buf)
    xs = pb + (prev - pb) * mu
    o1, o2, o3 = D_B, 2 * D_B, 3 * D_B
    o4 = o3 + LORA_W
    o5 = o4 + LORA_A
    r, k, v = xs[..., :o1], xs[..., o1:o2], xs[..., o2:o3]
    wd, ad, gd = xs[..., o3:o4], xs[..., o4:o5], xs[..., o5:]
    w_log = -jax.nn.softplus(-(w0 + jnp.tanh(wd) @ w2)) - 0.5
    decay = jnp.exp(-jnp.exp(w_log))
    a = jax.nn.sigmoid(a0 + ad @ a2)
    g = jax.nn.sigmoid(gd) @ g2
    heads = lambda z: z.reshape(Bn, T, N_HEAD_B, HEAD_DIM)
    kk = heads(k * k_k)
    kk = kk / jnp.maximum(jnp.linalg.norm(kk, axis=-1, keepdims=True), 1e-12)
    k = k * (1.0 + (a - 1.0) * k_a)
    r_h, k_h, v_h, w_h, a_h = heads(r), heads(k), heads(v), heads(decay), heads(a)

    def step(S, inp):
        w_t, kk_t, kka_t, k_t, v_t, r_t = inp
        S = (S * w_t[:, :, None, :]
             - jnp.einsum('bhvk,bhk->bhv', S, kk_t)[..., None] * kka_t[:, :, None, :]
             + v_t[..., None] * k_t[:, :, None, :])
        return S, jnp.einsum('bhvk,bhk->bhv', S, r_t)

    seq_first = lambda z: jnp.moveaxis(z, 1, 0)
    S_new, y = lax.scan(step, S0, (seq_first(w_h), seq_first(kk), seq_first(kk * a_h),
                                   seq_first(k_h), seq_first(v_h), seq_first(r_h)))
    y = jnp.moveaxis(y, 0, 1)
    y = head_norm(y, GN_EPS_B) * ln_w.reshape(N_HEAD_B, HEAD_DIM) + ln_b.reshape(N_HEAD_B, HEAD_DIM)
    bonus = jnp.sum(r_h * k_h * r_k, axis=-1, keepdims=True) * v_h
    out = (y + bonus).reshape(Bn, T, D_B) * g
    return out, shift_new, S_new


def mlstm_chunk(q, k, v, i_pre, logf, C0, n0, m0):
    L = q.shape[2]
    b = jnp.cumsum(logf, axis=-1)
    causal = jnp.tril(jnp.ones((L, L), dtype=bool))
    log_w = jnp.where(causal, b[..., :, None] - b[..., None, :] + i_pre[..., None, :], -jnp.inf)
    log_inter = b + m0[..., None]
    m = jnp.maximum(log_inter, jnp.max(log_w, axis=-1))
    w_inter = jnp.exp(log_inter - m)
    scores = jnp.einsum('bhtn,bhsn->bhts', q, k) * jnp.exp(log_w - m[..., None])
    num = (w_inter[..., None] * jnp.einsum('bhtn,bhnv->bhtv', q, C0)
           + jnp.einsum('bhts,bhsv->bhtv', scores, v))
    den = w_inter * jnp.einsum('bhtn,bhn->bht', q, n0) + jnp.sum(scores, axis=-1)
    h = num / jnp.maximum(jnp.abs(den), jnp.exp(-m))[..., None]
    log_end = b[..., -1:] - b + i_pre
    log_inter_end = b[..., -1] + m0
    m_new = jnp.maximum(log_inter_end, jnp.max(log_end, axis=-1))
    w_end = jnp.exp(log_end - m_new[..., None])
    w0_end = jnp.exp(log_inter_end - m_new)
    C_new = w0_end[..., None, None] * C0 + jnp.einsum('bhs,bhsn,bhsv->bhnv', w_end, k, v)
    n_new = w0_end[..., None] * n0 + jnp.einsum('bhs,bhsn->bhn', w_end, k)
    return h, (C_new, n_new, m_new)


def mlstm_scan(q, k, v, i_pre, logf, C0, n0, m0):
    Bn, H, T, N = q.shape
    L = CHUNK if T % CHUNK == 0 else T
    nc = T // L
    chunks = lambda z: jnp.moveaxis(z.reshape(Bn, H, nc, L, *z.shape[3:]), 2, 0)

    def step(carry, inp):
        h, carry = mlstm_chunk(*inp, *carry)
        return carry, h

    state, hs = lax.scan(step, (C0, n0, m0),
                         (chunks(q), chunks(k), chunks(v), chunks(i_pre), chunks(logf)))
    return jnp.moveaxis(hs, 0, 2).reshape(Bn, H, T, N), state


def mlstm_mix(pc, conv_buf, C0, n0, m0, conv_w, conv_b, wq, wk, w_if, b_if, gn_w):
    Bn, T, _ = pc.shape
    xc, vc, zc = pc[..., :D_C], pc[..., D_C:2 * D_C], pc[..., 2 * D_C:]
    xconv, conv_new = causal_conv(xc, conv_buf, conv_w, conv_b)
    xact = jax.nn.silu(xconv).reshape(Bn, T, N_HEAD_C, HEAD_DIM)
    q = jnp.einsum('bthd,hde->bthe', xact, wq)
    k = jnp.einsum('bthd,hde->bthe', xact, wk)
    v = vc.reshape(Bn, T, N_HEAD_C, HEAD_DIM)
    gates = jnp.concatenate([q.reshape(Bn, T, D_C), k.reshape(Bn, T, D_C), vc], axis=-1) @ w_if + b_if
    i_pre = gates[..., :N_HEAD_C]
    logf = jax.nn.log_sigmoid(gates[..., N_HEAD_C:])
    bh = lambda z: jnp.moveaxis(z, 1, 2)
    h, (C_new, n_new, m_new) = mlstm_scan(bh(q), bh(k) * HEAD_DIM ** -0.5, bh(v),
                                          bh(i_pre), bh(logf), C0, n0, m0)
    h = jnp.moveaxis(h, 2, 1)
    h = head_norm(h, GN_EPS_C) * gn_w.reshape(N_HEAD_C, HEAD_DIM)
    out = jax.nn.sigmoid(zc) * h.reshape(Bn, T, D_C)
    return out, conv_new, C_new, n_new, m_new


def layer(x, conv_a, h_lru, shift_b, wkv, conv_c, mem_c, mem_n, mem_m,
          norm1, w_in, conv_a_w, conv_a_b, lru_wr, lru_br, lru_wi, lru_bi, lru_lambda, norm_a,
          rwkv_mu, rwkv_w0, rwkv_w2, rwkv_a0, rwkv_a2, rwkv_g2, rwkv_kk, rwkv_ka, rwkv_rk,
          rwkv_lnw, rwkv_lnb,
          conv_c_w, conv_c_b, mlstm_wq, mlstm_wk, mlstm_wif, mlstm_bif, mlstm_gn,
          w_out, norm2, w_ffn_in, w_ffn_out):
    f32 = jnp.float32
    hn = rmsnorm(x, norm1)
    proj = jnp.einsum('btd,de->bte', hn, w_in).astype(f32)
    pa_x, pa_g = proj[..., :D_A], proj[..., D_A:2 * D_A]
    pb = proj[..., 2 * D_A:2 * D_A + D_B_IN]
    pc = proj[..., 2 * D_A + D_B_IN:]
    xa, conv_a_new = causal_conv(pa_x, conv_a.astype(f32), conv_a_w, conv_a_b)
    h, h_last = rg_lru(xa, h_lru.astype(f32), lru_wr, lru_br, lru_wi, lru_bi, lru_lambda)
    ya = rmsnorm(h, norm_a) * jax.nn.gelu(pa_g)
    yb, shift_new, S_new = rwkv7_mix(pb, shift_b.astype(f32), wkv.astype(f32), rwkv_mu, rwkv_w0,
                                     rwkv_w2, rwkv_a0, rwkv_a2, rwkv_g2, rwkv_kk, rwkv_ka, rwkv_rk,
                                     rwkv_lnw, rwkv_lnb)
    yc, conv_c_new, C_new, n_new, m_new = mlstm_mix(pc, conv_c.astype(f32), mem_c.astype(f32),
                                                    mem_n.astype(f32), mem_m.astype(f32),
                                                    conv_c_w, conv_c_b, mlstm_wq, mlstm_wk,
                                                    mlstm_wif, mlstm_bif, mlstm_gn)
    y_mix = jnp.concatenate([ya, yb, yc], axis=-1).astype(x.dtype)
    x = x + jnp.einsum('bte,ed->btd', y_mix, w_out)
    hn2 = rmsnorm(x, norm2)
    gu = jnp.einsum('btd,df->btf', hn2, w_ffn_in)
    x = x + jnp.einsum('btf,fd->btd', jax.nn.silu(gu[..., :D_FF]) * gu[..., D_FF:], w_ffn_out)
    return x, (conv_a_new, h_last, shift_new, S_new, conv_c_new, C_new, n_new, m_new)


def setup_inputs(seed: int = 0) -> dict:
    key = jax.random.key(seed)
    ks = iter(jax.random.split(key, 64))
    nrm = lambda shape, scale: scale * jax.random.normal(next(ks), shape, jnp.float32)
    uni = lambda shape, lo, hi: jax.random.uniform(next(ks), shape, jnp.float32, lo, hi)
    P = DEPTH
    lru_s = uni((P, D_A), 0.9, 0.999) ** (1.0 / LRU_C)
    f_bias = jnp.linspace(3.0, 6.0, N_HEAD_C, dtype=jnp.float32)
    return {
        'x_prompt': nrm((BATCH, SEQ, D_MODEL), 1.0),
        'x_sample': nrm((DEC_BATCH, DEC_SEQ, D_MODEL), 1.0),
        'state_conv_a': nrm((P, DEC_BATCH, CONV_W - 1, D_A), 1.0),
        'state_lru': nrm((P, DEC_BATCH, D_A), 0.5),
        'state_shift_b': nrm((P, DEC_BATCH, 1, D_B_IN), 1.0),
        'state_wkv': nrm((P, DEC_BATCH, N_HEAD_B, HEAD_DIM, HEAD_DIM), 0.3),
        'state_conv_c': nrm((P, DEC_BATCH, CONV_W - 1, D_C), 1.0),
        'state_mem_c': nrm((P, DEC_BATCH, N_HEAD_C, HEAD_DIM, HEAD_DIM), 0.1),
        'state_mem_n': nrm((P, DEC_BATCH, N_HEAD_C, HEAD_DIM), 0.1),
        'state_mem_m': nrm((P, DEC_BATCH, N_HEAD_C), 1.0),
        'norm1': 1.0 + nrm((P, D_MODEL), 0.02),
        'w_in': nrm((P, D_MODEL, D_IN), D_MODEL ** -0.5),
        'conv_a_w': nrm((P, CONV_W, D_A), CONV_W ** -0.5),
        'conv_a_b': nrm((P, D_A), 0.01),
        'lru_wr': nrm((P, N_BLK_A, HEAD_DIM, HEAD_DIM), HEAD_DIM ** -0.5),
        'lru_br': nrm((P, D_A), 0.01),
        'lru_wi': nrm((P, N_BLK_A, HEAD_DIM, HEAD_DIM), HEAD_DIM ** -0.5),
        'lru_bi': nrm((P, D_A), 0.01),
        'lru_lambda': jnp.log(lru_s) - jnp.log1p(-lru_s),
        'norm_a': 1.0 + nrm((P, D_A), 0.02),
        'rwkv_mu': uni((P, D_B_IN), 0.0, 1.0),
        'rwkv_w0': uni((P, D_B), -6.0, 1.0),
        'rwkv_w2': nrm((P, LORA_W, D_B), 0.1 * LORA_W ** -0.5),
        'rwkv_a0': nrm((P, D_B), 0.1),
        'rwkv_a2': nrm((P, LORA_A, D_B), 0.1 * LORA_A ** -0.5),
        'rwkv_g2': nrm((P, LORA_G, D_B), LORA_G ** -0.5),
        'rwkv_kk': 1.0 + nrm((P, D_B), 0.1),
        'rwkv_ka': 1.0 + nrm((P, D_B), 0.1),
        'rwkv_rk': nrm((P, N_HEAD_B, HEAD_DIM), 0.1),
        'rwkv_lnw': 1.0 + nrm((P, D_B), 0.02),
        'rwkv_lnb': nrm((P, D_B), 0.01),
        'conv_c_w': nrm((P, CONV_W, D_C), CONV_W ** -0.5),
        'conv_c_b': nrm((P, D_C), 0.01),
        'mlstm_wq': nrm((P, N_HEAD_C, HEAD_DIM, HEAD_DIM), HEAD_DIM ** -0.5),
        'mlstm_wk': nrm((P, N_HEAD_C, HEAD_DIM, HEAD_DIM), HEAD_DIM ** -0.5),
        'mlstm_wif': nrm((P, 3 * D_C, 2 * N_HEAD_C), (3 * D_C) ** -0.5),
        'mlstm_bif': jnp.concatenate([nrm((P, N_HEAD_C), 0.1), f_bias + nrm((P, N_HEAD_C), 0.1)], axis=-1),
        'mlstm_gn': 1.0 + nrm((P, D_C), 0.02),
        'w_out': nrm((P, D_MIX, D_MODEL), D_MIX ** -0.5),
        'norm2': 1.0 + nrm((P, D_MODEL), 0.02),
        'w_ffn_in': nrm((P, D_MODEL, 2 * D_FF), D_MODEL ** -0.5),
        'w_ffn_out': nrm((P, D_FF, D_MODEL), D_FF ** -0.5),
        'norm_f': 1.0 + nrm((D_MODEL,), 0.02),
    }


def reference(x_prompt, x_sample, state_conv_a, state_lru, state_shift_b, state_wkv, state_conv_c,
              state_mem_c, state_mem_n, state_mem_m,
              norm1, w_in, conv_a_w, conv_a_b, lru_wr, lru_br, lru_wi, lru_bi, lru_lambda, norm_a,
              rwkv_mu, rwkv_w0, rwkv_w2, rwkv_a0, rwkv_a2, rwkv_g2, rwkv_kk, rwkv_ka, rwkv_rk,
              rwkv_lnw, rwkv_lnb,
              conv_c_w, conv_c_b, mlstm_wq, mlstm_wk, mlstm_wif, mlstm_bif, mlstm_gn,
              w_out, norm2, w_ffn_in, w_ffn_out, norm_f):
    layer_w = (norm1, w_in, conv_a_w, conv_a_b, lru_wr, lru_br, lru_wi, lru_bi, lru_lambda, norm_a,
               rwkv_mu, rwkv_w0, rwkv_w2, rwkv_a0, rwkv_a2, rwkv_g2, rwkv_kk, rwkv_ka, rwkv_rk,
               rwkv_lnw, rwkv_lnb,
               conv_c_w, conv_c_b, mlstm_wq, mlstm_wk, mlstm_wif, mlstm_bif, mlstm_gn,
               w_out, norm2, w_ffn_in, w_ffn_out)

    def run(x, states):
        per_layer = []
        for l in range(DEPTH):
            x, st = layer(x, *[s[l] for s in states], *[w[l] for w in layer_w])
            per_layer.append(st)
        stacked = tuple(jnp.stack([st[j] for st in per_layer]) for j in range(len(states)))
        return rmsnorm(x, norm_f), stacked

    f32 = jnp.float32
    Bp = x_prompt.shape[0]
    zero_states = (jnp.zeros((DEPTH, Bp, CONV_W - 1, D_A), f32),
                   jnp.zeros((DEPTH, Bp, D_A), f32),
                   jnp.zeros((DEPTH, Bp, 1, D_B_IN), f32),
                   jnp.zeros((DEPTH, Bp, N_HEAD_B, HEAD_DIM, HEAD_DIM), f32),
                   jnp.zeros((DEPTH, Bp, CONV_W - 1, D_C), f32),
                   jnp.zeros((DEPTH, Bp, N_HEAD_C, HEAD_DIM, HEAD_DIM), f32),
                   jnp.zeros((DEPTH, Bp, N_HEAD_C, HEAD_DIM), f32),
                   jnp.zeros((DEPTH, Bp, N_HEAD_C), f32))
    y_prompt, (p_conv_a, p_lru, p_shift_b, p_wkv, p_conv_c, p_mem_c, p_mem_n, p_mem_m) = run(
        x_prompt, zero_states)
    y_sample, (s_conv_a, s_lru, s_shift_b, s_wkv, s_conv_c, s_mem_c, s_mem_n, s_mem_m) = run(
        x_sample, (state_conv_a, state_lru, state_shift_b, state_wkv, state_conv_c,
                   state_mem_c, state_mem_n, state_mem_m))
    return (y_prompt, y_sample,
            p_conv_a, p_lru, p_shift_b, p_wkv, p_conv_c, p_mem_c, p_mem_n, p_mem_m,
            s_conv_a, s_lru, s_shift_b, s_wkv, s_conv_c, s_mem_c, s_mem_n, s_mem_m)
```

```python
import functools
import math

import jax
import jax.numpy as jnp
from jax import lax
from jax.experimental import pallas as pl
from jax.experimental.pallas import tpu as pltpu

F32 = jnp.float32
BF16 = jnp.bfloat16

D_MODEL = 1024
DEPTH = 4
HEAD_DIM = 64
D_A = 256
N_BLK_A = 4
CONV_W = 4
LRU_C = 8.0
D_B = 384
N_HEAD_B = 6
LORA_W = 64
LORA_A = 64
LORA_G = 128
D_B_IN = 3 * D_B + LORA_W + LORA_A + LORA_G
D_C = 384
N_HEAD_C = 6
D_IN = 2 * D_A + D_B_IN + 3 * D_C
D_FF = 2816
RMS_EPS = 1e-6
GN_EPS_B = 64e-5
GN_EPS_C = 1e-6
MLSTM_CHUNK = 64

OFF_PB = 2 * D_A
OFF_PC = OFF_PB + D_B_IN
HIST = 8
GATE_W = 128
GATE_T = 16

(V_NORM1, V_CONV_A_B, V_LRU_B, V_LAM, V_NORM_A, V_MU, V_W0, V_A0, V_KK, V_KA, V_RK,
 V_LNW, V_LNB, V_CONV_C_B, V_BIF, V_GN, V_CONV_A_W, V_CONV_C_W) = (
     0, 1, 2, 3, 4, 5, 6, 7, 8, 9, 10, 11, 12, 13, 14, 15, 16, 20)
N_VEC = 24
VEC_W = D_B_IN

VMEM_LIMIT_MIXER = 48 * 1024 * 1024
VMEM_LIMIT_FFN = 56 * 1024 * 1024


def _bdot(a, b):
    return jnp.dot(a.astype(BF16), b.astype(BF16), preferred_element_type=F32)


def _bdot_nt(a, b):
    return lax.dot_general(a.astype(BF16), b.astype(BF16), (((1,), (1,)), ((), ())),
                           preferred_element_type=F32)


def _split(a, n):
    parts = []
    rem = a
    for i in range(n):
        p = rem.astype(BF16)
        parts.append(p)
        if i + 1 < n:
            rem = rem - p.astype(F32)
    return parts


def _edot_r(a, e, n=3):
    acc = None
    for p in _split(a, n):
        t = jnp.dot(p, e, preferred_element_type=F32)
        acc = t if acc is None else acc + t
    return acc


def _edot_l(e, a, n=3):
    acc = None
    for p in _split(a, n):
        t = jnp.dot(e, p, preferred_element_type=F32)
        acc = t if acc is None else acc + t
    return acc


def _xdot(a, b, nt=False):
    ah, al = _split(a, 2)
    bh, bl = _split(b, 2)
    if nt:
        f = lambda x, y: lax.dot_general(x, y, (((1,), (1,)), ((), ())),
                                         preferred_element_type=F32)
    else:
        f = lambda x, y: jnp.dot(x, y, preferred_element_type=F32)
    return f(ah, bh) + (f(ah, bl) + f(al, bh))


def _softplus(z):
    return jnp.maximum(z, 0.0) + jnp.log1p(jnp.exp(-jnp.abs(z)))


def _sigmoid(z):
    return jax.nn.sigmoid(z)


def _rms(x, g):
    return x * lax.rsqrt(jnp.mean(x * x, axis=-1, keepdims=True) + RMS_EPS) * g


def _iota2(shape, dim):
    return lax.broadcasted_iota(jnp.int32, shape, dim)


def _mixer_kernel(TT, L,
                  x_ref, ca_ref, lru_ref, sh_ref, wkv_ref, cc_ref, mc_ref, mn_ref, mm_ref,
                  vec_ref, w_in_ref, lruw_ref, w2_ref, a2_ref, g2_ref, wqk_ref, wif_ref,
                  wift_ref, bift_ref, w_out_ref,
                  xo_ref, ca_o, lru_o, sh_o, wkv_o, cc_o, mc_o, mn_o, mm_o,
                  proj_s, ymix_s, rk_s, ml_s, gate_s, yb_s, hc_s):
    j = pl.program_id(1)
    NC = TT // L
    LOG_L = int(math.log2(L))

    def vec(row, width):
        return vec_ref[0, row:row + 1, 0:width]

    @pl.when(j == 0)
    def _init():
        proj_s[0:HIST, :] = jnp.zeros((HIST, D_IN), F32)
        proj_s[HIST - 3:HIST, 0:D_A] = ca_ref[0]
        proj_s[HIST - 1:HIST, OFF_PB:OFF_PB + D_B_IN] = sh_ref[0]
        proj_s[HIST - 3:HIST, OFF_PC:OFF_PC + D_C] = cc_ref[0]
        lru_o[...] = lru_ref[...]
        wkv_o[...] = wkv_ref[...]
        mc_o[...] = mc_ref[...]
        mn_o[...] = mn_ref[...]
        mm_o[...] = mm_ref[...]

    x = x_ref[0]
    hn = _rms(x, vec(V_NORM1, D_MODEL))
    proj_s[HIST:HIST + TT, :] = jnp.dot(hn.astype(BF16), w_in_ref[0],
                                        preferred_element_type=F32)

    def taps(col0, width, w_row0):
        acc = None
        for t in range(CONV_W):
            r0 = HIST - (CONV_W - 1) + t
            term = vec(w_row0 + t, width) * proj_s[r0:r0 + TT, col0:col0 + width]
            acc = term if acc is None else acc + term
        return acc

    row_t = _iota2((TT, 1), 0)

    xa = taps(0, D_A, V_CONV_A_W) + vec(V_CONV_A_B, D_A)
    gates_a = _bdot(xa, lruw_ref[0]) + vec(V_LRU_B, 2 * D_A)
    gate_r = _sigmoid(gates_a[:, 0:D_A])
    gate_i = _sigmoid(gates_a[:, D_A:2 * D_A])
    log_a = (-LRU_C) * gate_r * _softplus(-vec(V_LAM, D_A))
    a_cum = jnp.exp(log_a)
    h_loc = jnp.sqrt(1.0 - jnp.exp(2.0 * log_a)) * (gate_i * xa)
    d = 1
    while d < TT:
        keep = row_t >= d
        a_sh = jnp.where(keep, pltpu.roll(a_cum, d, 0), 1.0)
        h_sh = jnp.where(keep, pltpu.roll(h_loc, d, 0), 0.0)
        h_loc = a_cum * h_sh + h_loc
        a_cum = a_cum * a_sh
        d *= 2
    h_lru = h_loc + a_cum * lru_o[0]
    lru_o[0] = h_lru[TT - 1:TT, :]
    pag = proj_s[HIST:HIST + TT, D_A:2 * D_A]
    gelu = 0.5 * pag * (1.0 + jnp.tanh(math.sqrt(2.0 / math.pi)
                                       * (pag + 0.044715 * (pag * pag * pag))))
    ymix_s[:, 0:D_A] = _rms(h_lru, vec(V_NORM_A, D_A)) * gelu

    seg_r = _iota2((D_B, D_B), 0) >> 6
    seg_c = _iota2((D_B, D_B), 1) >> 6
    ones_bd = jnp.where(seg_r == seg_c, 1.0, 0.0).astype(BF16)

    def segsum(z):
        return _edot_r(z, ones_bd)

    pb = proj_s[HIST:HIST + TT, OFF_PB:OFF_PB + D_B_IN]
    prev = proj_s[HIST - 1:HIST - 1 + TT, OFF_PB:OFF_PB + D_B_IN]
    xs = pb + (prev - pb) * vec(V_MU, D_B_IN)
    r_b = xs[:, 0:D_B]
    k_b = xs[:, D_B:2 * D_B]
    v_b = xs[:, 2 * D_B:3 * D_B]
    o3 = 3 * D_B
    wd = xs[:, o3:o3 + LORA_W]
    ad = xs[:, o3 + LORA_W:o3 + LORA_W + LORA_A]
    gd = xs[:, o3 + LORA_W + LORA_A:D_B_IN]
    w_log = -_softplus(-(vec(V_W0, D_B) + _bdot(jnp.tanh(wd), w2_ref[0]))) - 0.5
    logw = -jnp.exp(w_log)
    a_b = _sigmoid(vec(V_A0, D_B) + _bdot(ad, a2_ref[0]))
    g_b = _bdot(_sigmoid(gd), g2_ref[0])
    kk = k_b * vec(V_KK, D_B)
    kk = kk / jnp.maximum(jnp.sqrt(segsum(kk * kk)), 1e-12)
    k_b = k_b * (1.0 + (a_b - 1.0) * vec(V_KA, D_B))
    rk_s[0] = r_b
    rk_s[1] = k_b
    rk_s[2] = v_b
    rk_s[3] = kk
    rk_s[4] = kk * a_b
    rk_s[5] = logw
    bonus = segsum(r_b * k_b * vec(V_RK, D_B)) * v_b

    xconv = taps(OFF_PC, D_C, V_CONV_C_W) + vec(V_CONV_C_B, D_C)
    xact = xconv * _sigmoid(xconv)
    qk = _bdot(xact, wqk_ref[0])
    q_c = qk[:, 0:D_C]
    k_c = qk[:, D_C:2 * D_C]
    vc = proj_s[HIST:HIST + TT, OFF_PC + D_C:OFF_PC + 2 * D_C]
    gates_c = (_bdot(q_c, wif_ref[0, 0:D_C, :]) + _bdot(k_c, wif_ref[0, D_C:2 * D_C, :])
               + _bdot(vc, wif_ref[0, 2 * D_C:3 * D_C, :]) + vec(V_BIF, GATE_W))
    ml_s[0] = q_c
    ml_s[1] = k_c
    gate_s[0] = gates_c
    gate_s[1] = -_softplus(-gates_c)

    last3 = proj_s[HIST + TT - 3:HIST + TT, :]
    proj_s[HIST - 3:HIST, :] = last3
    ca_o[0] = last3[:, 0:D_A]
    sh_o[0] = last3[2:3, OFF_PB:OFF_PB + D_B_IN]
    cc_o[0] = last3[:, OFF_PC:OFF_PC + D_C]

    ti = _iota2((L, L), 0)
    si = _iota2((L, L), 1)
    tril_incl = ti >= si
    tril_strict = ti > si
    tril_incl_b = jnp.where(tril_incl, 1.0, 0.0).astype(BF16)
    triu_incl_b = jnp.where(ti <= si, 1.0, 0.0).astype(BF16)
    eye_l = jnp.where(ti == si, 1.0, 0.0)
    e64 = _iota2((HEAD_DIM, HEAD_DIM), 0) == _iota2((HEAD_DIM, HEAD_DIM), 1)
    lane6 = _iota2((1, N_HEAD_C), 1)

    def unit_lower_inverse(nmat):
        xinv = eye_l - jnp.where(((ti & 1) == 1) & (si == ti - 1), nmat, 0.0)
        for lg in range(1, LOG_L):
            bt = ti >> lg
            bs = si >> lg
            cm = jnp.where(((bt & 1) == 1) & (bs == bt - 1), nmat, 0.0)
            xinv = xinv - _xdot(_xdot(xinv, cm), xinv)
        return xinv

    def chunk(c, carry):
        r0 = pl.multiple_of(c * L, L)
        rows = pl.ds(r0, L)

        r_c = rk_s[0, rows, :]
        k2_c = rk_s[1, rows, :]
        v_c = rk_s[2, rows, :]
        kk_c = rk_s[3, rows, :]
        ka_c = rk_s[4, rows, :]
        lw_c = rk_s[5, rows, :]
        cl = _edot_l(tril_incl_b, lw_c)
        cl_end = cl[L - 1:L, :]
        w_end = jnp.exp(cl_end)
        w_inv = jnp.exp(-cl)
        q_t = kk_c * jnp.exp(cl - lw_c)
        k_t = k2_c * w_inv
        a_t = ka_c * w_inv
        r_t = r_c * jnp.exp(cl)
        to_end = jnp.exp(cl_end - cl)
        k_hat = k2_c * to_end
        a_hat = ka_c * to_end
        for h in range(N_HEAD_B):
            sl = slice(h * HEAD_DIM, (h + 1) * HEAD_DIM)
            qh, kh, ah, rh, vh = q_t[:, sl], k_t[:, sl], a_t[:, sl], r_t[:, sl], v_c[:, sl]
            n_m = jnp.where(tril_strict, _xdot(qh, ah, nt=True), 0.0)
            mk_m = jnp.where(tril_strict, _xdot(qh, kh, nt=True), 0.0)
            mrk_m = jnp.where(tril_incl, _xdot(rh, kh, nt=True), 0.0)
            mra_m = jnp.where(tril_incl, _xdot(rh, ah, nt=True), 0.0)
            xinv = unit_lower_inverse(n_m)
            p1 = _xdot(xinv, qh)
            p2 = _xdot(xinv, _xdot(mk_m, vh))
            rq = rh - _xdot(mra_m, p1)
            yc = _xdot(mrk_m, vh) - _xdot(mra_m, p2)
            ahat_t = a_hat[:, sl].T
            khat_t = k_hat[:, sl].T
            w_end_col = jnp.sum(jnp.where(e64, w_end[:, sl], 0.0), axis=1, keepdims=True)
            s_t = wkv_o[0, h]
            yb_s[rows, sl] = _xdot(rq, s_t) + yc
            g_s = _xdot(_xdot(ahat_t, p1), s_t)
            h_s = _xdot(khat_t, vh) - _xdot(ahat_t, p2)
            wkv_o[0, h] = w_end_col * s_t - g_s + h_s

        q_m = ml_s[0, rows, :]
        k_m = ml_s[1, rows, :]
        v_m = proj_s[pl.ds(HIST + r0, L), OFF_PC + D_C:OFF_PC + 2 * D_C]
        gt = (_bdot_nt(wift_ref[0, :, 0:D_C], q_m) + _bdot_nt(wift_ref[0, :, D_C:2 * D_C], k_m)
              + _bdot_nt(wift_ref[0, :, 2 * D_C:3 * D_C], v_m) + bift_ref[0])
        lf_t = -_softplus(-gt)
        b_row_all = _edot_r(lf_t, triu_incl_b)
        b_col_all = _edot_l(tril_incl_b, gate_s[1, rows, :])
        i_col_all = gate_s[0, rows, :]
        k_m = k_m * (HEAD_DIM ** -0.5)
        m_all = mm_o[0]
        for h in range(N_HEAD_C):
            sl = slice(h * HEAD_DIM, (h + 1) * HEAD_DIM)
            qh, kh, vh = q_m[:, sl], k_m[:, sl], v_m[:, sl]
            b_col = b_col_all[:, N_HEAD_C + h:N_HEAD_C + h + 1]
            i_col = i_col_all[:, h:h + 1]
            b_row = b_row_all[N_HEAD_C + h:N_HEAD_C + h + 1, :]
            i_row = gt[h:h + 1, :]
            m0 = m_all[:, h:h + 1]
            log_w = b_col - b_row + i_row
            row_max = jnp.max(jnp.where(tril_incl, log_w, -jnp.inf), axis=1, keepdims=True)
            log_inter = b_col + m0
            m_t = jnp.maximum(log_inter, row_max)
            w_inter = jnp.exp(log_inter - m_t)
            dmat = jnp.where(tril_incl, jnp.exp(log_w - m_t), 0.0)
            scores = _bdot_nt(qh, kh) * dmat
            c0 = mc_o[0, h]
            n0 = mn_o[0, h:h + 1, :]
            num = w_inter * _bdot(qh, c0) + _bdot(scores, vh)
            den = (w_inter * jnp.sum(qh * n0, axis=1, keepdims=True)
                   + jnp.sum(scores, axis=1, keepdims=True))
            hc_s[rows, sl] = num / jnp.maximum(jnp.abs(den), jnp.exp(-m_t))
            b_end = b_col[L - 1:L, :]
            log_end = b_end - b_col + i_col
            log_inter_end = b_end + m0
            m_new = jnp.maximum(log_inter_end, jnp.max(log_end, axis=0, keepdims=True))
            kw = kh * jnp.exp(log_end - m_new)
            w0_end = jnp.exp(log_inter_end - m_new)
            mc_o[0, h] = w0_end * c0 + _bdot(kw.T, vh)
            mn_o[0, h:h + 1, :] = w0_end * n0 + jnp.sum(kw, axis=0, keepdims=True)
            m_all = jnp.where(lane6 == h, m_new, m_all)
        mm_o[0] = m_all
        return carry

    lax.fori_loop(0, NC, chunk, 0)

    y_b = yb_s[...]
    y_cen = y_b - segsum(y_b) * (1.0 / HEAD_DIM)
    y_var = segsum(y_cen * y_cen) * (1.0 / HEAD_DIM)
    y_n = y_cen * lax.rsqrt(y_var + GN_EPS_B) * vec(V_LNW, D_B) + vec(V_LNB, D_B)
    ymix_s[:, D_A:D_A + D_B] = (y_n + bonus) * g_b

    h_c = hc_s[...]
    h_cen = h_c - segsum(h_c) * (1.0 / HEAD_DIM)
    h_var = segsum(h_cen * h_cen) * (1.0 / HEAD_DIM)
    zc = proj_s[HIST:HIST + TT, OFF_PC + 2 * D_C:OFF_PC + 3 * D_C]
    ymix_s[:, D_A + D_B:D_MODEL] = (_sigmoid(zc) * (h_cen * lax.rsqrt(h_var + GN_EPS_C)
                                                   * vec(V_GN, D_C)))

    xo_ref[0] = x_ref[0] + jnp.dot(ymix_s[...].astype(BF16), w_out_ref[0],
                                   preferred_element_type=F32)


def _mixer_call(layer, x, states, wts, TT, L):
    B, T, _ = x.shape
    NT = T // TT
    ca, lru, sh, wkv, cc, mc, mn, mm = states

    def state_spec(shape):
        nd = len(shape)
        return pl.BlockSpec((1,) + tuple(shape[1:]), lambda b, j: (b,) + (0,) * (nd - 1))

    def layer_spec(arr):
        nd = arr.ndim
        return pl.BlockSpec((1,) + tuple(arr.shape[1:]), lambda b, j: (layer,) + (0,) * (nd - 1))

    x_spec = pl.BlockSpec((1, TT, D_MODEL), lambda b, j: (b, j, 0))
    state_specs = [state_spec(s.shape) for s in states]
    w_names = ('vec', 'w_in', 'lruw', 'w2', 'a2', 'g2', 'wqk', 'wif', 'wift', 'bift', 'w_out')
    w_arrs = [wts[n] for n in w_names]

    out_shape = ([jax.ShapeDtypeStruct(x.shape, F32)]
                 + [jax.ShapeDtypeStruct(s.shape, F32) for s in states])
    scratch = [
        pltpu.VMEM((HIST + TT, D_IN), F32),
        pltpu.VMEM((TT, D_MODEL), F32),
        pltpu.VMEM((6, TT, D_B), F32),
        pltpu.VMEM((2, TT, D_C), F32),
        pltpu.VMEM((2, TT, GATE_W), F32),
        pltpu.VMEM((TT, D_B), F32),
        pltpu.VMEM((TT, D_C), F32),
    ]
    outs = pl.pallas_call(
        functools.partial(_mixer_kernel, TT, L),
        grid=(B, NT),
        in_specs=[x_spec] + state_specs + [layer_spec(a) for a in w_arrs],
        out_specs=[x_spec] + state_specs,
        out_shape=out_shape,
        scratch_shapes=scratch,
        compiler_params=pltpu.CompilerParams(
            dimension_semantics=("arbitrary", "arbitrary"),
            vmem_limit_bytes=VMEM_LIMIT_MIXER),
        name="mixer",
    )(x, *states, *w_arrs)
    return outs[0], tuple(outs[1:])


FF_SPLIT = 2
FF_CHUNK = D_FF // FF_SPLIT


def _ffn_kernel(final, x_ref, vec_ref, wi_ref, wo_ref, o_ref):
    x = x_ref[...]
    hn = _rms(x, vec_ref[0, 0:1, :]).astype(BF16)
    acc = x
    for f in range(FF_SPLIT):
        lo = f * FF_CHUNK
        gate = jnp.dot(hn, wi_ref[0, :, lo:lo + FF_CHUNK], preferred_element_type=F32)
        up = jnp.dot(hn, wi_ref[0, :, D_FF + lo:D_FF + lo + FF_CHUNK],
                     preferred_element_type=F32)
        act = (gate * _sigmoid(gate) * up).astype(BF16)
        acc = acc + jnp.dot(act, wo_ref[0, lo:lo + FF_CHUNK, :], preferred_element_type=F32)
    if final:
        acc = _rms(acc, vec_ref[0, 1:2, :])
    o_ref[...] = acc


def _ffn_call(layer, x2d, wts, TM, final):
    M = x2d.shape[0]
    return pl.pallas_call(
        functools.partial(_ffn_kernel, final),
        grid=(M // TM,),
        in_specs=[
            pl.BlockSpec((TM, D_MODEL), lambda i: (i, 0)),
            pl.BlockSpec((1, 8, D_MODEL), lambda i: (layer, 0, 0)),
            pl.BlockSpec((1, D_MODEL, 2 * D_FF), lambda i: (layer, 0, 0)),
            pl.BlockSpec((1, D_FF, D_MODEL), lambda i: (layer, 0, 0)),
        ],
        out_specs=pl.BlockSpec((TM, D_MODEL), lambda i: (i, 0)),
        out_shape=jax.ShapeDtypeStruct(x2d.shape, F32),
        compiler_params=pltpu.CompilerParams(
            dimension_semantics=("arbitrary",),
            vmem_limit_bytes=VMEM_LIMIT_FFN),
        name="ffn",
    )(x2d, wts['ffn_vec'], wts['w_ffn_in'], wts['w_ffn_out'])


def _block_diag(w):
    P, n, d, e = w.shape
    out = jnp.zeros((P, n * d, n * e), w.dtype)
    for i in range(n):
        out = out.at[:, i * d:(i + 1) * d, i * e:(i + 1) * e].set(w[:, i])
    return out


def _pack_rows(P, rows):
    table = jnp.zeros((P, N_VEC, VEC_W), F32)
    for r, arr in rows:
        arr = arr.reshape(P, -1, arr.shape[-1]).astype(F32)
        table = table.at[:, r:r + arr.shape[1], 0:arr.shape[2]].set(arr)
    return table


def _run_stream(x, states, wts, TT, L, TM):
    B, T, _ = x.shape
    new_states = []
    for l in range(DEPTH):
        st_l = tuple(s[l] for s in states)
        x, st_new = _mixer_call(l, x, st_l, wts, TT, L)
        new_states.append(st_new)
        x = _ffn_call(l, x.reshape(B * T, D_MODEL), wts, TM, l == DEPTH - 1).reshape(B, T, D_MODEL)
    stacked = tuple(jnp.stack([st[i] for st in new_states]) for i in range(len(states)))
    return x, stacked


def kernel(x_prompt, x_sample, state_conv_a, state_lru, state_shift_b, state_wkv, state_conv_c, state_mem_c, state_mem_n, state_mem_m, norm1, w_in, conv_a_w, conv_a_b, lru_wr, lru_br, lru_wi, lru_bi, lru_lambda, norm_a, rwkv_mu, rwkv_w0, rwkv_w2, rwkv_a0, rwkv_a2, rwkv_g2, rwkv_kk, rwkv_ka, rwkv_rk, rwkv_lnw, rwkv_lnb, conv_c_w, conv_c_b, mlstm_wq, mlstm_wk, mlstm_wif, mlstm_bif, mlstm_gn, w_out, norm2, w_ffn_in, w_ffn_out, norm_f):
    P = DEPTH
    wif_pad = jnp.zeros((P, 3 * D_C, GATE_W), F32).at[:, :, 0:2 * N_HEAD_C].set(mlstm_wif)
    wift = jnp.zeros((P, GATE_T, 3 * D_C), F32).at[:, 0:2 * N_HEAD_C, :].set(
        jnp.swapaxes(mlstm_wif, 1, 2))
    bift = jnp.zeros((P, GATE_T, 1), F32).at[:, 0:2 * N_HEAD_C, 0].set(mlstm_bif)
    wts = {
        'vec': _pack_rows(P, [
            (V_NORM1, norm1), (V_CONV_A_B, conv_a_b),
            (V_LRU_B, jnp.concatenate([lru_br, lru_bi], axis=-1)), (V_LAM, lru_lambda),
            (V_NORM_A, norm_a), (V_MU, rwkv_mu), (V_W0, rwkv_w0), (V_A0, rwkv_a0),
            (V_KK, rwkv_kk), (V_KA, rwkv_ka), (V_RK, rwkv_rk.reshape(P, D_B)),
            (V_LNW, rwkv_lnw), (V_LNB, rwkv_lnb), (V_CONV_C_B, conv_c_b),
            (V_BIF, mlstm_bif), (V_GN, mlstm_gn),
            (V_CONV_A_W, conv_a_w), (V_CONV_C_W, conv_c_w)]),
        'w_in': w_in.astype(BF16),
        'lruw': jnp.concatenate([_block_diag(lru_wr), _block_diag(lru_wi)], axis=-1).astype(BF16),
        'w2': rwkv_w2.astype(BF16),
        'a2': rwkv_a2.astype(BF16),
        'g2': rwkv_g2.astype(BF16),
        'wqk': jnp.concatenate([_block_diag(mlstm_wq), _block_diag(mlstm_wk)], axis=-1).astype(BF16),
        'wif': wif_pad.astype(BF16),
        'wift': wift.astype(BF16),
        'bift': bift,
        'w_out': w_out.astype(BF16),
        'ffn_vec': jnp.zeros((P, 8, D_MODEL), F32).at[:, 0].set(norm2).at[:, 1].set(
            jnp.broadcast_to(norm_f, (P, D_MODEL))),
        'w_ffn_in': w_ffn_in.astype(BF16),
        'w_ffn_out': w_ffn_out.astype(BF16),
    }

    def to_kernel_layout(conv_a, lru, shift, wkv, conv_c, mem_c, mem_n, mem_m):
        Bn = lru.shape[1]
        return (conv_a, lru.reshape(P, Bn, 1, D_A), shift, jnp.swapaxes(wkv, -1, -2), conv_c,
                mem_c, mem_n, mem_m.reshape(P, Bn, 1, N_HEAD_C))

    def from_kernel_layout(conv_a, lru, shift, wkv, conv_c, mem_c, mem_n, mem_m):
        Bn = lru.shape[1]
        return (conv_a, lru.reshape(P, Bn, D_A), shift, jnp.swapaxes(wkv, -1, -2), conv_c,
                mem_c, mem_n, mem_m.reshape(P, Bn, N_HEAD_C))

    Bp, Tp, _ = x_prompt.shape
    Bs, Ts, _ = x_sample.shape
    zero_states = (jnp.zeros((P, Bp, CONV_W - 1, D_A), F32),
                   jnp.zeros((P, Bp, D_A), F32),
                   jnp.zeros((P, Bp, 1, D_B_IN), F32),
                   jnp.zeros((P, Bp, N_HEAD_B, HEAD_DIM, HEAD_DIM), F32),
                   jnp.zeros((P, Bp, CONV_W - 1, D_C), F32),
                   jnp.zeros((P, Bp, N_HEAD_C, HEAD_DIM, HEAD_DIM), F32),
                   jnp.zeros((P, Bp, N_HEAD_C, HEAD_DIM), F32),
                   jnp.zeros((P, Bp, N_HEAD_C), F32))
    Lp = MLSTM_CHUNK if Tp % MLSTM_CHUNK == 0 else Tp
    Ls = MLSTM_CHUNK if Ts % MLSTM_CHUNK == 0 else Ts
    TTp = 256 if Tp % 256 == 0 else Lp
    y_prompt, p_states = _run_stream(x_prompt, to_kernel_layout(*zero_states), wts, TTp, Lp, 256)
    y_sample, s_states = _run_stream(
        x_sample, to_kernel_layout(state_conv_a, state_lru, state_shift_b, state_wkv,
                                   state_conv_c, state_mem_c, state_mem_n, state_mem_m),
        wts, Ls, Ls, Bs * Ts)
    return (y_prompt, y_sample) + from_kernel_layout(*p_states) + from_kernel_layout(*s_states)
```

```python
import functools
import math

import jax
import jax.numpy as jnp
from jax import lax
from jax.experimental import pallas as pl
from jax.experimental.pallas import tpu as pltpu

F32 = jnp.float32
BF16 = jnp.bfloat16

D_MODEL = 1024
DEPTH = 4
HEAD_DIM = 64
D_A = 256
N_BLK_A = 4
CONV_W = 4
LRU_C = 8.0
D_B = 384
N_HEAD_B = 6
LORA_W = 64
LORA_A = 64
LORA_G = 128
D_B_IN = 3 * D_B + LORA_W + LORA_A + LORA_G
D_C = 384
N_HEAD_C = 6
D_IN = 2 * D_A + D_B_IN + 3 * D_C
D_FF = 2816
RMS_EPS = 1e-6
GN_EPS_B = 64e-5
GN_EPS_C = 1e-6
MLSTM_CHUNK = 64

OFF_PB = 2 * D_A
OFF_PC = OFF_PB + D_B_IN
HIST = 8
GATE_W = 128
GATE_T = 16

(V_NORM1, V_CONV_A_B, V_LRU_B, V_LAM, V_NORM_A, V_MU, V_W0, V_A0, V_KK, V_KA, V_RK,
 V_LNW, V_LNB, V_CONV_C_B, V_BIF, V_GN, V_CONV_A_W, V_CONV_C_W) = (
     0, 1, 2, 3, 4, 5, 6, 7, 8, 9, 10, 11, 12, 13, 14, 15, 16, 20)
N_VEC = 24
VEC_W = D_B_IN

VMEM_LIMIT_MIXER = 48 * 1024 * 1024
VMEM_LIMIT_FFN = 56 * 1024 * 1024


def _bdot(a, b):
    return jnp.dot(a.astype(BF16), b.astype(BF16), preferred_element_type=F32)


def _bdot_nt(a, b):
    return lax.dot_general(a.astype(BF16), b.astype(BF16), (((1,), (1,)), ((), ())),
                           preferred_element_type=F32)


def _split(a, n):
    parts = []
    rem = a
    for i in range(n):
        p = rem.astype(BF16)
        parts.append(p)
        if i + 1 < n:
            rem = rem - p.astype(F32)
    return parts


def _edot_r(a, e, n=3):
    acc = None
    for p in _split(a, n):
        t = jnp.dot(p, e, preferred_element_type=F32)
        acc = t if acc is None else acc + t
    return acc


def _edot_l(e, a, n=3):
    acc = None
    for p in _split(a, n):
        t = jnp.dot(e, p, preferred_element_type=F32)
        acc = t if acc is None else acc + t
    return acc


def _xdot(a, b, nt=False):
    return _bdot_nt(a, b) if nt else _bdot(a, b)


def _softplus(z):
    return jnp.maximum(z, 0.0) + jnp.log1p(jnp.exp(-jnp.abs(z)))


def _sigmoid(z):
    return jax.nn.sigmoid(z)


def _rms(x, g):
    return x * lax.rsqrt(jnp.mean(x * x, axis=-1, keepdims=True) + RMS_EPS) * g


def _iota2(shape, dim):
    return lax.broadcasted_iota(jnp.int32, shape, dim)


def _mixer_kernel(TT, L,
                  x_ref, ca_ref, lru_ref, sh_ref, wkv_ref, cc_ref, mc_ref, mn_ref, mm_ref,
                  vec_ref, w_in_ref, lruw_ref, w2_ref, a2_ref, g2_ref, wqk_ref, wif_ref,
                  wift_ref, bift_ref, w_out_ref,
                  xo_ref, ca_o, lru_o, sh_o, wkv_o, cc_o, mc_o, mn_o, mm_o,
                  proj_s, ymix_s, rk_s, ml_s, gate_s, yb_s, hc_s):
    j = pl.program_id(1)
    NC = TT // L
    LOG_L = int(math.log2(L))

    def vec(row, width):
        return vec_ref[0, row:row + 1, 0:width]

    @pl.when(j == 0)
    def _init():
        proj_s[0:HIST, :] = jnp.zeros((HIST, D_IN), F32)
        proj_s[HIST - 3:HIST, 0:D_A] = ca_ref[0]
        proj_s[HIST - 1:HIST, OFF_PB:OFF_PB + D_B_IN] = sh_ref[0]
        proj_s[HIST - 3:HIST, OFF_PC:OFF_PC + D_C] = cc_ref[0]
        lru_o[...] = lru_ref[...]
        wkv_o[...] = wkv_ref[...]
        mc_o[...] = mc_ref[...]
        mn_o[...] = mn_ref[...]
        mm_o[...] = mm_ref[...]

    x = x_ref[0]
    hn = _rms(x, vec(V_NORM1, D_MODEL))
    proj_s[HIST:HIST + TT, :] = jnp.dot(hn.astype(BF16), w_in_ref[0],
                                        preferred_element_type=F32)

    def taps(col0, width, w_row0):
        acc = None
        for t in range(CONV_W):
            r0 = HIST - (CONV_W - 1) + t
            term = vec(w_row0 + t, width) * proj_s[r0:r0 + TT, col0:col0 + width]
            acc = term if acc is None else acc + term
        return acc

    row_t = _iota2((TT, 1), 0)

    xa = taps(0, D_A, V_CONV_A_W) + vec(V_CONV_A_B, D_A)
    gates_a = _bdot(xa, lruw_ref[0]) + vec(V_LRU_B, 2 * D_A)
    gate_r = _sigmoid(gates_a[:, 0:D_A])
    gate_i = _sigmoid(gates_a[:, D_A:2 * D_A])
    log_a = (-LRU_C) * gate_r * _softplus(-vec(V_LAM, D_A))
    a_cum = jnp.exp(log_a)
    h_loc = jnp.sqrt(1.0 - jnp.exp(2.0 * log_a)) * (gate_i * xa)
    d = 1
    while d < TT:
        keep = row_t >= d
        a_sh = jnp.where(keep, pltpu.roll(a_cum, d, 0), 1.0)
        h_sh = jnp.where(keep, pltpu.roll(h_loc, d, 0), 0.0)
        h_loc = a_cum * h_sh + h_loc
        a_cum = a_cum * a_sh
        d *= 2
    h_lru = h_loc + a_cum * lru_o[0]
    lru_o[0] = h_lru[TT - 1:TT, :]
    pag = proj_s[HIST:HIST + TT, D_A:2 * D_A]
    gelu = 0.5 * pag * (1.0 + jnp.tanh(math.sqrt(2.0 / math.pi)
                                       * (pag + 0.044715 * (pag * pag * pag))))
    ymix_s[:, 0:D_A] = _rms(h_lru, vec(V_NORM_A, D_A)) * gelu

    seg_r = _iota2((D_B, D_B), 0) >> 6
    seg_c = _iota2((D_B, D_B), 1) >> 6
    ones_bd = jnp.where(seg_r == seg_c, 1.0, 0.0).astype(BF16)

    def segsum(z):
        return _edot_r(z, ones_bd)

    pb = proj_s[HIST:HIST + TT, OFF_PB:OFF_PB + D_B_IN]
    prev = proj_s[HIST - 1:HIST - 1 + TT, OFF_PB:OFF_PB + D_B_IN]
    xs = pb + (prev - pb) * vec(V_MU, D_B_IN)
    r_b = xs[:, 0:D_B]
    k_b = xs[:, D_B:2 * D_B]
    v_b = xs[:, 2 * D_B:3 * D_B]
    o3 = 3 * D_B
    wd = xs[:, o3:o3 + LORA_W]
    ad = xs[:, o3 + LORA_W:o3 + LORA_W + LORA_A]
    gd = xs[:, o3 + LORA_W + LORA_A:D_B_IN]
    w_log = -_softplus(-(vec(V_W0, D_B) + _bdot(jnp.tanh(wd), w2_ref[0]))) - 0.5
    logw = -jnp.exp(w_log)
    a_b = _sigmoid(vec(V_A0, D_B) + _bdot(ad, a2_ref[0]))
    g_b = _bdot(_sigmoid(gd), g2_ref[0])
    kk = k_b * vec(V_KK, D_B)
    kk = kk / jnp.maximum(jnp.sqrt(segsum(kk * kk)), 1e-12)
    k_b = k_b * (1.0 + (a_b - 1.0) * vec(V_KA, D_B))
    rk_s[0] = r_b
    rk_s[1] = k_b
    rk_s[2] = v_b
    rk_s[3] = kk
    rk_s[4] = kk * a_b
    rk_s[5] = logw
    bonus = segsum(r_b * k_b * vec(V_RK, D_B)) * v_b

    xconv = taps(OFF_PC, D_C, V_CONV_C_W) + vec(V_CONV_C_B, D_C)
    xact = xconv * _sigmoid(xconv)
    qk = _bdot(xact, wqk_ref[0])
    q_c = qk[:, 0:D_C]
    k_c = qk[:, D_C:2 * D_C]
    vc = proj_s[HIST:HIST + TT, OFF_PC + D_C:OFF_PC + 2 * D_C]
    gates_c = (_bdot(q_c, wif_ref[0, 0:D_C, :]) + _bdot(k_c, wif_ref[0, D_C:2 * D_C, :])
               + _bdot(vc, wif_ref[0, 2 * D_C:3 * D_C, :]) + vec(V_BIF, GATE_W))
    ml_s[0] = q_c
    ml_s[1] = k_c
    gate_s[0] = gates_c
    gate_s[1] = -_softplus(-gates_c)

    last3 = proj_s[HIST + TT - 3:HIST + TT, :]
    proj_s[HIST - 3:HIST, :] = last3
    ca_o[0] = last3[:, 0:D_A]
    sh_o[0] = last3[2:3, OFF_PB:OFF_PB + D_B_IN]
    cc_o[0] = last3[:, OFF_PC:OFF_PC + D_C]

    ti = _iota2((L, L), 0)
    si = _iota2((L, L), 1)
    tril_incl = ti >= si
    tril_strict = ti > si
    tril_incl_b = jnp.where(tril_incl, 1.0, 0.0).astype(BF16)
    triu_incl_b = jnp.where(ti <= si, 1.0, 0.0).astype(BF16)
    eye_l = jnp.where(ti == si, 1.0, 0.0)
    e64 = _iota2((HEAD_DIM, HEAD_DIM), 0) == _iota2((HEAD_DIM, HEAD_DIM), 1)
    lane6 = _iota2((1, N_HEAD_C), 1)

    def unit_lower_inverse(nmats):
        lvl1 = ((ti & 1) == 1) & (si == ti - 1)
        xinv = [eye_l - jnp.where(lvl1, n, 0.0) for n in nmats]
        for lg in range(1, LOG_L):
            bt = ti >> lg
            bs = si >> lg
            lvl = ((bt & 1) == 1) & (bs == bt - 1)
            xc = [_xdot(x, jnp.where(lvl, n, 0.0)) for x, n in zip(xinv, nmats)]
            xinv = [x - _xdot(z, x) for x, z in zip(xinv, xc)]
        return xinv

    def chunk(c, carry):
        r0 = pl.multiple_of(c * L, L)
        rows = pl.ds(r0, L)

        r_c = rk_s[0, rows, :]
        k2_c = rk_s[1, rows, :]
        v_c = rk_s[2, rows, :]
        kk_c = rk_s[3, rows, :]
        ka_c = rk_s[4, rows, :]
        lw_c = rk_s[5, rows, :]
        cl = _edot_l(tril_incl_b, lw_c)
        cl_end = cl[L - 1:L, :]
        w_end = jnp.exp(cl_end)
        w_inv = jnp.exp(-cl)
        q_t = kk_c * jnp.exp(cl - lw_c)
        k_t = k2_c * w_inv
        a_t = ka_c * w_inv
        r_t = r_c * jnp.exp(cl)
        to_end = jnp.exp(cl_end - cl)
        k_hat = k2_c * to_end
        a_hat = ka_c * to_end
        HB = range(N_HEAD_B)
        sls = [slice(h * HEAD_DIM, (h + 1) * HEAD_DIM) for h in HB]
        qh = [q_t[:, s] for s in sls]
        kh = [k_t[:, s] for s in sls]
        ah = [a_t[:, s] for s in sls]
        rh = [r_t[:, s] for s in sls]
        vh = [v_c[:, s] for s in sls]
        n_m = [jnp.where(tril_strict, _xdot(qh[h], ah[h], nt=True), 0.0) for h in HB]
        mk_m = [jnp.where(tril_strict, _xdot(qh[h], kh[h], nt=True), 0.0) for h in HB]
        mrk_m = [jnp.where(tril_incl, _xdot(rh[h], kh[h], nt=True), 0.0) for h in HB]
        mra_m = [jnp.where(tril_incl, _xdot(rh[h], ah[h], nt=True), 0.0) for h in HB]
        mkv = [_xdot(mk_m[h], vh[h]) for h in HB]
        mrkv = [_xdot(mrk_m[h], vh[h]) for h in HB]
        xinv = unit_lower_inverse(n_m)
        p1 = [_xdot(xinv[h], qh[h]) for h in HB]
        p2 = [_xdot(xinv[h], mkv[h]) for h in HB]
        rq = [rh[h] - _xdot(mra_m[h], p1[h]) for h in HB]
        yc = [mrkv[h] - _xdot(mra_m[h], p2[h]) for h in HB]
        ahat_t = [a_hat[:, s].T for s in sls]
        khat_t = [k_hat[:, s].T for s in sls]
        g_m = [_xdot(ahat_t[h], p1[h]) for h in HB]
        h_m = [_xdot(khat_t[h], vh[h]) - _xdot(ahat_t[h], p2[h]) for h in HB]
        w_end_col = [jnp.sum(jnp.where(e64, w_end[:, s], 0.0), axis=1, keepdims=True)
                     for s in sls]
        s_t = [wkv_o[0, h] for h in HB]
        y_h = [_xdot(rq[h], s_t[h]) + yc[h] for h in HB]
        s_new = [w_end_col[h] * s_t[h] - _xdot(g_m[h], s_t[h]) + h_m[h] for h in HB]
        for h in HB:
            yb_s[rows, sls[h]] = y_h[h]
            wkv_o[0, h] = s_new[h]

        q_m = ml_s[0, rows, :]
        k_m = ml_s[1, rows, :]
        v_m = proj_s[pl.ds(HIST + r0, L), OFF_PC + D_C:OFF_PC + 2 * D_C]
        gt = (_bdot_nt(wift_ref[0, :, 0:D_C], q_m) + _bdot_nt(wift_ref[0, :, D_C:2 * D_C], k_m)
              + _bdot_nt(wift_ref[0, :, 2 * D_C:3 * D_C], v_m) + bift_ref[0])
        lf_t = -_softplus(-gt)
        b_row_all = _edot_r(lf_t, triu_incl_b)
        b_col_all = _edot_l(tril_incl_b, gate_s[1, rows, :])
        i_col_all = gate_s[0, rows, :]
        k_m = k_m * (HEAD_DIM ** -0.5)
        m_all = mm_o[0]
        for h in range(N_HEAD_C):
            sl = slice(h * HEAD_DIM, (h + 1) * HEAD_DIM)
            qh, kh, vh = q_m[:, sl], k_m[:, sl], v_m[:, sl]
            b_col = b_col_all[:, N_HEAD_C + h:N_HEAD_C + h + 1]
            i_col = i_col_all[:, h:h + 1]
            b_row = b_row_all[N_HEAD_C + h:N_HEAD_C + h + 1, :]
            i_row = gt[h:h + 1, :]
            m0 = m_all[:, h:h + 1]
            log_w = b_col - b_row + i_row
            row_max = jnp.max(jnp.where(tril_incl, log_w, -jnp.inf), axis=1, keepdims=True)
            log_inter = b_col + m0
            m_t = jnp.maximum(log_inter, row_max)
            w_inter = jnp.exp(log_inter - m_t)
            dmat = jnp.where(tril_incl, jnp.exp(log_w - m_t), 0.0)
            scores = _bdot_nt(qh, kh) * dmat
            c0 = mc_o[0, h]
            n0 = mn_o[0, h:h + 1, :]
            num = w_inter * _bdot(qh, c0) + _bdot(scores, vh)
            den = (w_inter * jnp.sum(qh * n0, axis=1, keepdims=True)
                   + jnp.sum(scores, axis=1, keepdims=True))
            hc_s[rows, sl] = num / jnp.maximum(jnp.abs(den), jnp.exp(-m_t))
            b_end = b_col[L - 1:L, :]
            log_end = b_end - b_col + i_col
            log_inter_end = b_end + m0
            m_new = jnp.maximum(log_inter_end, jnp.max(log_end, axis=0, keepdims=True))
            kw = kh * jnp.exp(log_end - m_new)
            w0_end = jnp.exp(log_inter_end - m_new)
            mc_o[0, h] = w0_end * c0 + _bdot(kw.T, vh)
            mn_o[0, h:h + 1, :] = w0_end * n0 + jnp.sum(kw, axis=0, keepdims=True)
            m_all = jnp.where(lane6 == h, m_new, m_all)
        mm_o[0] = m_all
        return carry

    lax.fori_loop(0, NC, chunk, 0)

    y_b = yb_s[...]
    y_cen = y_b - segsum(y_b) * (1.0 / HEAD_DIM)
    y_var = segsum(y_cen * y_cen) * (1.0 / HEAD_DIM)
    y_n = y_cen * lax.rsqrt(y_var + GN_EPS_B) * vec(V_LNW, D_B) + vec(V_LNB, D_B)
    ymix_s[:, D_A:D_A + D_B] = (y_n + bonus) * g_b

    h_c = hc_s[...]
    h_cen = h_c - segsum(h_c) * (1.0 / HEAD_DIM)
    h_var = segsum(h_cen * h_cen) * (1.0 / HEAD_DIM)
    zc = proj_s[HIST:HIST + TT, OFF_PC + 2 * D_C:OFF_PC + 3 * D_C]
    ymix_s[:, D_A + D_B:D_MODEL] = (_sigmoid(zc) * (h_cen * lax.rsqrt(h_var + GN_EPS_C)
                                                   * vec(V_GN, D_C)))

    xo_ref[0] = x_ref[0] + jnp.dot(ymix_s[...].astype(BF16), w_out_ref[0],
                                   preferred_element_type=F32)


def _mixer_call(layer, x, states, wts, TT, L):
    B, T, _ = x.shape
    NT = T // TT
    ca, lru, sh, wkv, cc, mc, mn, mm = states

    def state_spec(shape):
        nd = len(shape)
        return pl.BlockSpec((1,) + tuple(shape[1:]), lambda b, j: (b,) + (0,) * (nd - 1))

    def layer_spec(arr):
        nd = arr.ndim
        return pl.BlockSpec((1,) + tuple(arr.shape[1:]), lambda b, j: (layer,) + (0,) * (nd - 1))

    x_spec = pl.BlockSpec((1, TT, D_MODEL), lambda b, j: (b, j, 0))
    state_specs = [state_spec(s.shape) for s in states]
    w_names = ('vec', 'w_in', 'lruw', 'w2', 'a2', 'g2', 'wqk', 'wif', 'wift', 'bift', 'w_out')
    w_arrs = [wts[n] for n in w_names]

    out_shape = ([jax.ShapeDtypeStruct(x.shape, F32)]
                 + [jax.ShapeDtypeStruct(s.shape, F32) for s in states])
    scratch = [
        pltpu.VMEM((HIST + TT, D_IN), F32),
        pltpu.VMEM((TT, D_MODEL), F32),
        pltpu.VMEM((6, TT, D_B), F32),
        pltpu.VMEM((2, TT, D_C), F32),
        pltpu.VMEM((2, TT, GATE_W), F32),
        pltpu.VMEM((TT, D_B), F32),
        pltpu.VMEM((TT, D_C), F32),
    ]
    outs = pl.pallas_call(
        functools.partial(_mixer_kernel, TT, L),
        grid=(B, NT),
        in_specs=[x_spec] + state_specs + [layer_spec(a) for a in w_arrs],
        out_specs=[x_spec] + state_specs,
        out_shape=out_shape,
        scratch_shapes=scratch,
        compiler_params=pltpu.CompilerParams(
            dimension_semantics=("arbitrary", "arbitrary"),
            vmem_limit_bytes=VMEM_LIMIT_MIXER),
        name="mixer",
    )(x, *states, *w_arrs)
    return outs[0], tuple(outs[1:])


FF_SPLIT = 2
FF_CHUNK = D_FF // FF_SPLIT


def _ffn_kernel(final, x_ref, vec_ref, wi_ref, wo_ref, o_ref):
    x = x_ref[...]
    hn = _rms(x, vec_ref[0, 0:1, :]).astype(BF16)
    acc = x
    for f in range(FF_SPLIT):
        lo = f * FF_CHUNK
        gate = jnp.dot(hn, wi_ref[0, :, lo:lo + FF_CHUNK], preferred_element_type=F32)
        up = jnp.dot(hn, wi_ref[0, :, D_FF + lo:D_FF + lo + FF_CHUNK],
                     preferred_element_type=F32)
        act = (gate * _sigmoid(gate) * up).astype(BF16)
        acc = acc + jnp.dot(act, wo_ref[0, lo:lo + FF_CHUNK, :], preferred_element_type=F32)
    if final:
        acc = _rms(acc, vec_ref[0, 1:2, :])
    o_ref[...] = acc


def _ffn_call(layer, x2d, wts, TM, final):
    M = x2d.shape[0]
    return pl.pallas_call(
        functools.partial(_ffn_kernel, final),
        grid=(M // TM,),
        in_specs=[
            pl.BlockSpec((TM, D_MODEL), lambda i: (i, 0)),
            pl.BlockSpec((1, 8, D_MODEL), lambda i: (layer, 0, 0)),
            pl.BlockSpec((1, D_MODEL, 2 * D_FF), lambda i: (layer, 0, 0)),
            pl.BlockSpec((1, D_FF, D_MODEL), lambda i: (layer, 0, 0)),
        ],
        out_specs=pl.BlockSpec((TM, D_MODEL), lambda i: (i, 0)),
        out_shape=jax.ShapeDtypeStruct(x2d.shape, F32),
        compiler_params=pltpu.CompilerParams(
            dimension_semantics=("arbitrary",),
            vmem_limit_bytes=VMEM_LIMIT_FFN),
        name="ffn",
    )(x2d, wts['ffn_vec'], wts['w_ffn_in'], wts['w_ffn_out'])


def _block_diag(w):
    P, n, d, e = w.shape
    out = jnp.zeros((P, n * d, n * e), w.dtype)
    for i in range(n):
        out = out.at[:, i * d:(i + 1) * d, i * e:(i + 1) * e].set(w[:, i])
    return out


def _pack_rows(P, rows):
    table = jnp.zeros((P, N_VEC, VEC_W), F32)
    for r, arr in rows:
        arr = arr.reshape(P, -1, arr.shape[-1]).astype(F32)
        table = table.at[:, r:r + arr.shape[1], 0:arr.shape[2]].set(arr)
    return table


def _run_stream(x, states, wts, TT, L, TM):
    B, T, _ = x.shape
    new_states = []
    for l in range(DEPTH):
        st_l = tuple(s[l] for s in states)
        x, st_new = _mixer_call(l, x, st_l, wts, TT, L)
        new_states.append(st_new)
        x = _ffn_call(l, x.reshape(B * T, D_MODEL), wts, TM, l == DEPTH - 1).reshape(B, T, D_MODEL)
    stacked = tuple(jnp.stack([st[i] for st in new_states]) for i in range(len(states)))
    return x, stacked


def kernel(x_prompt, x_sample, state_conv_a, state_lru, state_shift_b, state_wkv, state_conv_c, state_mem_c, state_mem_n, state_mem_m, norm1, w_in, conv_a_w, conv_a_b, lru_wr, lru_br, lru_wi, lru_bi, lru_lambda, norm_a, rwkv_mu, rwkv_w0, rwkv_w2, rwkv_a0, rwkv_a2, rwkv_g2, rwkv_kk, rwkv_ka, rwkv_rk, rwkv_lnw, rwkv_lnb, conv_c_w, conv_c_b, mlstm_wq, mlstm_wk, mlstm_wif, mlstm_bif, mlstm_gn, w_out, norm2, w_ffn_in, w_ffn_out, norm_f):
    P = DEPTH
    wif_pad = jnp.zeros((P, 3 * D_C, GATE_W), F32).at[:, :, 0:2 * N_HEAD_C].set(mlstm_wif)
    wift = jnp.zeros((P, GATE_T, 3 * D_C), F32).at[:, 0:2 * N_HEAD_C, :].set(
        jnp.swapaxes(mlstm_wif, 1, 2))
    bift = jnp.zeros((P, GATE_T, 1), F32).at[:, 0:2 * N_HEAD_C, 0].set(mlstm_bif)
    wts = {
        'vec': _pack_rows(P, [
            (V_NORM1, norm1), (V_CONV_A_B, conv_a_b),
            (V_LRU_B, jnp.concatenate([lru_br, lru_bi], axis=-1)), (V_LAM, lru_lambda),
            (V_NORM_A, norm_a), (V_MU, rwkv_mu), (V_W0, rwkv_w0), (V_A0, rwkv_a0),
            (V_KK, rwkv_kk), (V_KA, rwkv_ka), (V_RK, rwkv_rk.reshape(P, D_B)),
            (V_LNW, rwkv_lnw), (V_LNB, rwkv_lnb), (V_CONV_C_B, conv_c_b),
            (V_BIF, mlstm_bif), (V_GN, mlstm_gn),
            (V_CONV_A_W, conv_a_w), (V_CONV_C_W, conv_c_w)]),
        'w_in': w_in.astype(BF16),
        'lruw': jnp.concatenate([_block_diag(lru_wr), _block_diag(lru_wi)], axis=-1).astype(BF16),
        'w2': rwkv_w2.astype(BF16),
        'a2': rwkv_a2.astype(BF16),
        'g2': rwkv_g2.astype(BF16),
        'wqk': jnp.concatenate([_block_diag(mlstm_wq), _block_diag(mlstm_wk)], axis=-1).astype(BF16),
        'wif': wif_pad.astype(BF16),
        'wift': wift.astype(BF16),
        'bift': bift,
        'w_out': w_out.astype(BF16),
        'ffn_vec': jnp.zeros((P, 8, D_MODEL), F32).at[:, 0].set(norm2).at[:, 1].set(
            jnp.broadcast_to(norm_f, (P, D_MODEL))),
        'w_ffn_in': w_ffn_in.astype(BF16),
        'w_ffn_out': w_ffn_out.astype(BF16),
    }

    def to_kernel_layout(conv_a, lru, shift, wkv, conv_c, mem_c, mem_n, mem_m):
        Bn = lru.shape[1]
        return (conv_a, lru.reshape(P, Bn, 1, D_A), shift, jnp.swapaxes(wkv, -1, -2), conv_c,
                mem_c, mem_n, mem_m.reshape(P, Bn, 1, N_HEAD_C))

    def from_kernel_layout(conv_a, lru, shift, wkv, conv_c, mem_c, mem_n, mem_m):
        Bn = lru.shape[1]
        return (conv_a, lru.reshape(P, Bn, D_A), shift, jnp.swapaxes(wkv, -1, -2), conv_c,
                mem_c, mem_n, mem_m.reshape(P, Bn, N_HEAD_C))

    Bp, Tp, _ = x_prompt.shape
    Bs, Ts, _ = x_sample.shape
    zero_states = (jnp.zeros((P, Bp, CONV_W - 1, D_A), F32),
                   jnp.zeros((P, Bp, D_A), F32),
                   jnp.zeros((P, Bp, 1, D_B_IN), F32),
                   jnp.zeros((P, Bp, N_HEAD_B, HEAD_DIM, HEAD_DIM), F32),
                   jnp.zeros((P, Bp, CONV_W - 1, D_C), F32),
                   jnp.zeros((P, Bp, N_HEAD_C, HEAD_DIM, HEAD_DIM), F32),
                   jnp.zeros((P, Bp, N_HEAD_C, HEAD_DIM), F32),
                   jnp.zeros((P, Bp, N_HEAD_C), F32))
    Lp = MLSTM_CHUNK if Tp % MLSTM_CHUNK == 0 else Tp
    Ls = MLSTM_CHUNK if Ts % MLSTM_CHUNK == 0 else Ts
    TTp = 256 if Tp % 256 == 0 else Lp
    y_prompt, p_states = _run_stream(x_prompt, to_kernel_layout(*zero_states), wts, TTp, Lp, 256)
    y_sample, s_states = _run_stream(
        x_sample, to_kernel_layout(state_conv_a, state_lru, state_shift_b, state_wkv,
                                   state_conv_c, state_mem_c, state_mem_n, state_mem_m),
        wts, Ls, Ls, Bs * Ts)
    return (y_prompt, y_sample) + from_kernel_layout(*p_states) + from_kernel_layout(*s_states)
```

```python
import functools
import math

import jax
import jax.numpy as jnp
from jax import lax
from jax.experimental import pallas as pl
from jax.experimental.pallas import tpu as pltpu

F32 = jnp.float32
BF16 = jnp.bfloat16

D_MODEL = 1024
DEPTH = 4
HEAD_DIM = 64
D_A = 256
N_BLK_A = 4
CONV_W = 4
LRU_C = 8.0
D_B = 384
N_HEAD_B = 6
LORA_W = 64
LORA_A = 64
LORA_G = 128
D_B_IN = 3 * D_B + LORA_W + LORA_A + LORA_G
D_C = 384
N_HEAD_C = 6
D_IN = 2 * D_A + D_B_IN + 3 * D_C
D_FF = 2816
RMS_EPS = 1e-6
GN_EPS_B = 64e-5
GN_EPS_C = 1e-6
MLSTM_CHUNK = 64

OFF_PB = 2 * D_A
OFF_PC = OFF_PB + D_B_IN
HIST = 8
GATE_W = 128

(V_NORM1, V_CONV_A_B, V_LRU_B, V_LAM, V_NORM_A, V_MU, V_W0, V_A0, V_KK, V_KA, V_RK,
 V_LNW, V_LNB, V_CONV_C_B, V_BIF, V_GN, V_CONV_A_W, V_CONV_C_W, V_BIF2) = (
     0, 1, 2, 3, 4, 5, 6, 7, 8, 9, 10, 11, 12, 13, 14, 15, 16, 20, 24)
N_VEC = 32
VEC_W = D_B_IN

VMEM_LIMIT_MIXER = 48 * 1024 * 1024
VMEM_LIMIT_FFN = 56 * 1024 * 1024


def _bdot(a, b):
    return jnp.dot(a.astype(BF16), b.astype(BF16), preferred_element_type=F32)


def _bdot_nt(a, b):
    return lax.dot_general(a.astype(BF16), b.astype(BF16), (((1,), (1,)), ((), ())),
                           preferred_element_type=F32)


def _split(a, n):
    parts = []
    rem = a
    for i in range(n):
        p = rem.astype(BF16)
        parts.append(p)
        if i + 1 < n:
            rem = rem - p.astype(F32)
    return parts


def _edot_r(a, e, n=3):
    acc = None
    for p in _split(a, n):
        t = jnp.dot(p, e, preferred_element_type=F32)
        acc = t if acc is None else acc + t
    return acc


def _edot_l(e, a, n=3):
    acc = None
    for p in _split(a, n):
        t = jnp.dot(e, p, preferred_element_type=F32)
        acc = t if acc is None else acc + t
    return acc


def _xdot(a, b, nt=False):
    return _bdot_nt(a, b) if nt else _bdot(a, b)


def _softplus(z):
    return jnp.maximum(z, 0.0) + jnp.log1p(jnp.exp(-jnp.abs(z)))


def _sigmoid(z):
    return jax.nn.sigmoid(z)


def _rms(x, g):
    return x * lax.rsqrt(jnp.mean(x * x, axis=-1, keepdims=True) + RMS_EPS) * g


def _iota2(shape, dim):
    return lax.broadcasted_iota(jnp.int32, shape, dim)


def _mixer_kernel(TT, L,
                  x_ref, ca_ref, lru_ref, sh_ref, wkv_ref, cc_ref, mc_ref, mn_ref, mm_ref,
                  vec_ref, w_in_ref, lruw_ref, w2_ref, a2_ref, g2_ref, wqk_ref, wif_ref, w_out_ref,
                  xo_ref, ca_o, lru_o, sh_o, wkv_o, cc_o, mc_o, mn_o, mm_o,
                  proj_s, ymix_s, rk_s, ml_s, yb_s, cbd_s, nrow_s, mrow_s):
    j = pl.program_id(1)
    NC = TT // L
    LOG_L = int(math.log2(L))
    CH = range(NC)
    HB = range(N_HEAD_B)
    sls = [slice(h * HEAD_DIM, (h + 1) * HEAD_DIM) for h in HB]
    rws = [slice(c * L, (c + 1) * L) for c in CH]

    def vec(row, width):
        return vec_ref[0, row:row + 1, 0:width]

    @pl.when(j == 0)
    def _init():
        proj_s[0:HIST, :] = jnp.zeros((HIST, D_IN), F32)
        proj_s[HIST - 3:HIST, 0:D_A] = ca_ref[0]
        proj_s[HIST - 1:HIST, OFF_PB:OFF_PB + D_B_IN] = sh_ref[0]
        proj_s[HIST - 3:HIST, OFF_PC:OFF_PC + D_C] = cc_ref[0]
        lru_o[...] = lru_ref[...]
        wkv_o[...] = wkv_ref[...]
        cbd_s[...] = jnp.zeros((D_C, D_C), F32)
        mrow_s[...] = jnp.zeros((1, GATE_W), F32)
        for h in range(N_HEAD_C):
            cbd_s[sls[h], sls[h]] = mc_ref[0, h]
            nrow_s[:, sls[h]] = mn_ref[0, h:h + 1, :]
        mrow_s[:, 0:N_HEAD_C] = mm_ref[0]

    x = x_ref[0]
    hn = _rms(x, vec(V_NORM1, D_MODEL))
    proj_s[HIST:HIST + TT, :] = jnp.dot(hn.astype(BF16), w_in_ref[0],
                                        preferred_element_type=F32)

    def taps(col0, width, w_row0):
        acc = None
        for t in range(CONV_W):
            r0 = HIST - (CONV_W - 1) + t
            term = vec(w_row0 + t, width) * proj_s[r0:r0 + TT, col0:col0 + width]
            acc = term if acc is None else acc + term
        return acc

    row_t = _iota2((TT, 1), 0)

    xa = taps(0, D_A, V_CONV_A_W) + vec(V_CONV_A_B, D_A)
    gates_a = _bdot(xa, lruw_ref[0]) + vec(V_LRU_B, 2 * D_A)
    gate_r = _sigmoid(gates_a[:, 0:D_A])
    gate_i = _sigmoid(gates_a[:, D_A:2 * D_A])
    log_a = (-LRU_C) * gate_r * _softplus(-vec(V_LAM, D_A))
    a_cum = jnp.exp(log_a)
    h_loc = jnp.sqrt(1.0 - jnp.exp(2.0 * log_a)) * (gate_i * xa)
    d = 1
    while d < TT:
        keep = row_t >= d
        a_sh = jnp.where(keep, pltpu.roll(a_cum, d, 0), 1.0)
        h_sh = jnp.where(keep, pltpu.roll(h_loc, d, 0), 0.0)
        h_loc = a_cum * h_sh + h_loc
        a_cum = a_cum * a_sh
        d *= 2
    h_lru = h_loc + a_cum * lru_o[0]
    lru_o[0] = h_lru[TT - 1:TT, :]
    pag = proj_s[HIST:HIST + TT, D_A:2 * D_A]
    gelu = 0.5 * pag * (1.0 + jnp.tanh(math.sqrt(2.0 / math.pi)
                                       * (pag + 0.044715 * (pag * pag * pag))))
    ymix_s[:, 0:D_A] = _rms(h_lru, vec(V_NORM_A, D_A)) * gelu

    seg_r = _iota2((D_B, D_B), 0) >> 6
    seg_c = _iota2((D_B, D_B), 1) >> 6
    bd_mask = seg_r == seg_c
    ones_bd = jnp.where(bd_mask, 1.0, 0.0).astype(BF16)
    seg_ones = jnp.where((_iota2((D_C, GATE_W), 0) >> 6) == _iota2((D_C, GATE_W), 1),
                         1.0, 0.0).astype(BF16)
    expand_m = jnp.where(_iota2((GATE_W, D_C), 0) == (_iota2((GATE_W, D_C), 1) >> 6),
                         1.0, 0.0).astype(BF16)
    tt_r = _iota2((TT, TT), 0)
    tt_c = _iota2((TT, TT), 1)
    same_chunk = (tt_r >> LOG_L) == (tt_c >> LOG_L)
    blk_tril = jnp.where(same_chunk & (tt_r >= tt_c), 1.0, 0.0).astype(BF16)
    blk_ones = jnp.where(same_chunk, 1.0, 0.0).astype(BF16)

    def segsum(z):
        return _edot_r(z, ones_bd)

    def expand(z):
        return _edot_r(z, expand_m)

    pb = proj_s[HIST:HIST + TT, OFF_PB:OFF_PB + D_B_IN]
    prev = proj_s[HIST - 1:HIST - 1 + TT, OFF_PB:OFF_PB + D_B_IN]
    xs = pb + (prev - pb) * vec(V_MU, D_B_IN)
    r_b = xs[:, 0:D_B]
    k_b = xs[:, D_B:2 * D_B]
    v_b = xs[:, 2 * D_B:3 * D_B]
    o3 = 3 * D_B
    wd = xs[:, o3:o3 + LORA_W]
    ad = xs[:, o3 + LORA_W:o3 + LORA_W + LORA_A]
    gd = xs[:, o3 + LORA_W + LORA_A:D_B_IN]
    w_log = -_softplus(-(vec(V_W0, D_B) + _bdot(jnp.tanh(wd), w2_ref[0]))) - 0.5
    logw = -jnp.exp(w_log)
    a_b = _sigmoid(vec(V_A0, D_B) + _bdot(ad, a2_ref[0]))
    g_b = _bdot(_sigmoid(gd), g2_ref[0])
    kk = k_b * vec(V_KK, D_B)
    kk = kk / jnp.maximum(jnp.sqrt(segsum(kk * kk)), 1e-12)
    k_b = k_b * (1.0 + (a_b - 1.0) * vec(V_KA, D_B))
    bonus = segsum(r_b * k_b * vec(V_RK, D_B)) * v_b
    ka_b = kk * a_b
    cl = _edot_l(blk_tril, logw)
    ctot = _edot_l(blk_ones, logw)
    w_inv = jnp.exp(-cl)
    to_end = jnp.exp(ctot - cl)
    rk_s[0] = kk * jnp.exp(cl - logw)
    rk_s[1] = k_b * w_inv
    rk_s[2] = ka_b * w_inv
    rk_s[3] = r_b * jnp.exp(cl)
    rk_s[4] = v_b
    rk_s[5] = k_b * to_end
    rk_s[6] = ka_b * to_end
    w_end = [jnp.exp(ctot[c * L:c * L + 1, :]) for c in CH]

    xconv = taps(OFF_PC, D_C, V_CONV_C_W) + vec(V_CONV_C_B, D_C)
    xact = xconv * _sigmoid(xconv)
    qk = _bdot(xact, wqk_ref[0])
    q_c = qk[:, 0:D_C]
    k_c = qk[:, D_C:2 * D_C]
    vc = proj_s[HIST:HIST + TT, OFF_PC + D_C:OFF_PC + 2 * D_C]
    gates_c = (_bdot(q_c, wif_ref[0, 0:D_C, :]) + _bdot(k_c, wif_ref[0, D_C:2 * D_C, :])
               + _bdot(vc, wif_ref[0, 2 * D_C:3 * D_C, :]))
    i_pre = gates_c[:, 0:GATE_W] + vec(V_BIF, GATE_W)
    logf = -_softplus(-(gates_c[:, GATE_W:2 * GATE_W] + vec(V_BIF2, GATE_W)))
    ml_s[0] = q_c
    ml_s[1] = k_c * (HEAD_DIM ** -0.5)

    last3 = proj_s[HIST + TT - 3:HIST + TT, :]
    proj_s[HIST - 3:HIST, :] = last3
    ca_o[0] = last3[:, 0:D_A]
    sh_o[0] = last3[2:3, OFF_PB:OFF_PB + D_B_IN]
    cc_o[0] = last3[:, OFF_PC:OFF_PC + D_C]

    ti = _iota2((L, L), 0)
    si = _iota2((L, L), 1)
    tril_incl = ti >= si
    tril_strict = ti > si
    eye_l = jnp.where(ti == si, 1.0, 0.0)
    e64 = _iota2((HEAD_DIM, HEAD_DIM), 0) == _iota2((HEAD_DIM, HEAD_DIM), 1)
    CHH = [(c, h) for c in CH for h in HB]

    def opnd(i):
        return {(c, h): rk_s[i, rws[c], sls[h]] for c, h in CHH}

    qh, kh, ah, rh, vh = opnd(0), opnd(1), opnd(2), opnd(3), opnd(4)
    n_m = {p: jnp.where(tril_strict, _xdot(qh[p], ah[p], nt=True), 0.0) for p in CHH}
    mk_m = {p: jnp.where(tril_strict, _xdot(qh[p], kh[p], nt=True), 0.0) for p in CHH}
    mrk_m = {p: jnp.where(tril_incl, _xdot(rh[p], kh[p], nt=True), 0.0) for p in CHH}
    mra_m = {p: jnp.where(tril_incl, _xdot(rh[p], ah[p], nt=True), 0.0) for p in CHH}
    mkv = {p: _xdot(mk_m[p], vh[p]) for p in CHH}
    mrkv = {p: _xdot(mrk_m[p], vh[p]) for p in CHH}
    lvl1 = ((ti & 1) == 1) & (si == ti - 1)
    xinv = {p: eye_l - jnp.where(lvl1, n_m[p], 0.0) for p in CHH}
    for lg in range(1, LOG_L):
        bt = ti >> lg
        bs = si >> lg
        lvl = ((bt & 1) == 1) & (bs == bt - 1)
        xc = {p: _xdot(xinv[p], jnp.where(lvl, n_m[p], 0.0)) for p in CHH}
        xinv = {p: xinv[p] - _xdot(xc[p], xinv[p]) for p in CHH}
    p1 = {p: _xdot(xinv[p], qh[p]) for p in CHH}
    p2 = {p: _xdot(xinv[p], mkv[p]) for p in CHH}
    rq = {p: rh[p] - _xdot(mra_m[p], p1[p]) for p in CHH}
    yc = {p: mrkv[p] - _xdot(mra_m[p], p2[p]) for p in CHH}
    khat_t = [rk_s[5, rws[c], :].T for c in CH]
    ahat_t = [rk_s[6, rws[c], :].T for c in CH]
    g_m = {(c, h): _xdot(ahat_t[c][sls[h], :], p1[c, h]) for c, h in CHH}
    h_m = {(c, h): _xdot(khat_t[c][sls[h], :], vh[c, h]) - _xdot(ahat_t[c][sls[h], :], p2[c, h])
           for c, h in CHH}
    w_end_col = {(c, h): jnp.sum(jnp.where(e64, w_end[c][:, sls[h]], 0.0), axis=1, keepdims=True)
                 for c, h in CHH}
    s_t = [wkv_o[0, h] for h in HB]
    for c in CH:
        y_h = [_xdot(rq[c, h], s_t[h]) + yc[c, h] for h in HB]
        s_t = [w_end_col[c, h] * s_t[h] - _xdot(g_m[c, h], s_t[h]) + h_m[c, h] for h in HB]
        for h in HB:
            yb_s[rws[c], sls[h]] = y_h[h]
    for h in HB:
        wkv_o[0, h] = s_t[h]

    y_b = yb_s[...]
    y_cen = y_b - segsum(y_b) * (1.0 / HEAD_DIM)
    y_var = segsum(y_cen * y_cen) * (1.0 / HEAD_DIM)
    y_n = y_cen * lax.rsqrt(y_var + GN_EPS_B) * vec(V_LNW, D_B) + vec(V_LNB, D_B)
    ymix_s[:, D_A:D_A + D_B] = (y_n + bonus) * g_b

    b_col = _edot_l(blk_tril, logf)
    b_tot = _edot_l(blk_ones, logf)
    g_col = i_pre - b_col
    g_max = g_col
    pos = row_t & (L - 1)
    d = 1
    while d < L:
        g_max = jnp.where(pos >= d, jnp.maximum(g_max, pltpu.roll(g_max, d, 0)), g_max)
        d *= 2
    m_loc = b_col + g_max
    g_last = [g_max[c * L + L - 1:c * L + L, :] for c in CH]
    b_end = [b_tot[c * L:c * L + 1, :] for c in CH]
    m_loc_end = [b_end[c] + g_last[c] for c in CH]
    w_end_loc = jnp.concatenate([jnp.exp(g_col[rws[c], :] - g_last[c]) for c in CH], axis=0)
    kw = ml_s[1] * expand(w_end_loc)
    s_of_lane = _iota2((L, N_HEAD_C * L), 1) & (L - 1)
    t_of_row = _iota2((L, N_HEAD_C * L), 0)
    causal_t = t_of_row >= s_of_lane
    eye_t = t_of_row == s_of_lane
    bd_l = (_iota2((N_HEAD_C * L, D_C), 0) >> LOG_L) == (_iota2((N_HEAD_C * L, D_C), 1) >> 6)
    seg_ones_l = jnp.where((_iota2((N_HEAD_C * L, GATE_W), 0) >> LOG_L)
                           == _iota2((N_HEAD_C * L, GATE_W), 1), 1.0, 0.0).astype(BF16)
    expand_l = jnp.where(_iota2((GATE_W, N_HEAD_C * L), 0)
                         == (_iota2((GATE_W, N_HEAD_C * L), 1) >> LOG_L), 1.0, 0.0).astype(BF16)
    g_exp_l = _edot_r(g_col, expand_l)
    g_max_exp_l = _edot_r(g_max, expand_l)

    p_loc, den_loc, c_loc, n_loc = [], [], [], []
    for c in CH:
        q_cc = ml_s[0, rws[c], :]
        k_cc = ml_s[1, rws[c], :]
        v_cc = proj_s[HIST + c * L:HIST + (c + 1) * L, OFF_PC + D_C:OFF_PC + 2 * D_C]
        k_bd = jnp.where(bd_l, jnp.concatenate([k_cc] * N_HEAD_C, axis=0), 0.0)
        v_bd = jnp.where(bd_l, jnp.concatenate([v_cc] * N_HEAD_C, axis=0), 0.0)
        g_flat = jnp.sum(jnp.where(eye_t, g_exp_l[rws[c], :], 0.0), axis=0, keepdims=True)
        dmat = jnp.where(causal_t, jnp.exp(jnp.minimum(g_flat - g_max_exp_l[rws[c], :], 0.0)), 0.0)
        scores = _bdot_nt(q_cc, k_bd) * dmat
        p_loc.append(_bdot(scores, v_bd))
        den_loc.append(_edot_r(scores, seg_ones_l, n=2))
        kw_c = kw[rws[c], :]
        c_loc.append(jnp.where(bd_mask, _bdot(kw_c.T, v_cc), 0.0))
        n_loc.append(jnp.sum(kw_c, axis=0, keepdims=True))

    m_in = []
    m_cur = mrow_s[...]
    for c in CH:
        m_in.append(m_cur)
        m_cur = jnp.maximum(b_end[c] + m_cur, m_loc_end[c])
    m_out = m_in[1:] + [m_cur]
    mrow_s[...] = m_cur
    scale_rows = jnp.concatenate(
        [jnp.exp(b_end[c] + m_in[c] - m_out[c]) for c in CH]
        + [jnp.exp(m_loc_end[c] - m_out[c]) for c in CH], axis=0)
    scale_exp = expand(scale_rows)
    c_in, n_in = [], []
    c_cur = cbd_s[...]
    n_cur = nrow_s[...]
    for c in CH:
        c_in.append(c_cur)
        n_in.append(n_cur)
        w0 = scale_exp[c:c + 1, :]
        wl = scale_exp[NC + c:NC + c + 1, :]
        c_cur = c_cur * w0 + c_loc[c] * wl
        n_cur = n_cur * w0 + n_loc[c] * wl
    cbd_s[...] = c_cur
    nrow_s[...] = n_cur
    for h in range(N_HEAD_C):
        mc_o[0, h] = c_cur[sls[h], sls[h]]
        mn_o[0, h:h + 1, :] = n_cur[:, sls[h]]
    mm_o[0] = m_cur[:, 0:N_HEAD_C]

    m0_b = jnp.concatenate([jnp.broadcast_to(m_in[c], (L, GATE_W)) for c in CH], axis=0)
    log_inter = b_col + m0_b
    m_t = jnp.maximum(log_inter, m_loc)
    w_inter = jnp.exp(log_inter - m_t)
    w_local = jnp.exp(m_loc - m_t)
    q_all = ml_s[0]
    qc0 = jnp.concatenate([_bdot(q_all[rws[c], :], c_in[c]) for c in CH], axis=0)
    qn0 = jnp.concatenate([_edot_r(q_all[rws[c], :] * n_in[c], seg_ones) for c in CH], axis=0)
    num = expand(w_inter) * qc0 + expand(w_local) * jnp.concatenate(p_loc, axis=0)
    den = w_inter * qn0 + w_local * jnp.concatenate(den_loc, axis=0)
    h_c = num * expand(1.0 / jnp.maximum(jnp.abs(den), jnp.exp(-m_t)))

    h_cen = h_c - segsum(h_c) * (1.0 / HEAD_DIM)
    h_var = segsum(h_cen * h_cen) * (1.0 / HEAD_DIM)
    zc = proj_s[HIST:HIST + TT, OFF_PC + 2 * D_C:OFF_PC + 3 * D_C]
    ymix_s[:, D_A + D_B:D_MODEL] = (_sigmoid(zc) * (h_cen * lax.rsqrt(h_var + GN_EPS_C)
                                                   * vec(V_GN, D_C)))

    xo_ref[0] = x_ref[0] + jnp.dot(ymix_s[...].astype(BF16), w_out_ref[0],
                                   preferred_element_type=F32)


def _mixer_call(layer, x, states, wts, TT, L):
    B, T, _ = x.shape
    NT = T // TT

    def state_spec(shape):
        nd = len(shape)
        return pl.BlockSpec((1,) + tuple(shape[1:]), lambda b, j: (b,) + (0,) * (nd - 1))

    def layer_spec(arr):
        nd = arr.ndim
        return pl.BlockSpec((1,) + tuple(arr.shape[1:]), lambda b, j: (layer,) + (0,) * (nd - 1))

    x_spec = pl.BlockSpec((1, TT, D_MODEL), lambda b, j: (b, j, 0))
    state_specs = [state_spec(s.shape) for s in states]
    w_names = ('vec', 'w_in', 'lruw', 'w2', 'a2', 'g2', 'wqk', 'wif', 'w_out')
    w_arrs = [wts[n] for n in w_names]

    out_shape = ([jax.ShapeDtypeStruct(x.shape, F32)]
                 + [jax.ShapeDtypeStruct(s.shape, F32) for s in states])
    scratch = [
        pltpu.VMEM((HIST + TT, D_IN), F32),
        pltpu.VMEM((TT, D_MODEL), F32),
        pltpu.VMEM((7, TT, D_B), F32),
        pltpu.VMEM((2, TT, D_C), F32),
        pltpu.VMEM((TT, D_B), F32),
        pltpu.VMEM((D_C, D_C), F32),
        pltpu.VMEM((1, D_C), F32),
        pltpu.VMEM((1, GATE_W), F32),
    ]
    outs = pl.pallas_call(
        functools.partial(_mixer_kernel, TT, L),
        grid=(B, NT),
        in_specs=[x_spec] + state_specs + [layer_spec(a) for a in w_arrs],
        out_specs=[x_spec] + state_specs,
        out_shape=out_shape,
        scratch_shapes=scratch,
        compiler_params=pltpu.CompilerParams(
            dimension_semantics=("arbitrary", "arbitrary"),
            vmem_limit_bytes=VMEM_LIMIT_MIXER),
        name="mixer",
    )(x, *states, *w_arrs)
    return outs[0], tuple(outs[1:])


FF_SPLIT = 2
FF_CHUNK = D_FF // FF_SPLIT


def _ffn_kernel(final, x_ref, vec_ref, wi_ref, wo_ref, o_ref):
    x = x_ref[...]
    hn = _rms(x, vec_ref[0, 0:1, :]).astype(BF16)
    acc = x
    for f in range(FF_SPLIT):
        lo = f * FF_CHUNK
        gate = jnp.dot(hn, wi_ref[0, :, lo:lo + FF_CHUNK], preferred_element_type=F32)
        up = jnp.dot(hn, wi_ref[0, :, D_FF + lo:D_FF + lo + FF_CHUNK],
                     preferred_element_type=F32)
        act = (gate * _sigmoid(gate) * up).astype(BF16)
        acc = acc + jnp.dot(act, wo_ref[0, lo:lo + FF_CHUNK, :], preferred_element_type=F32)
    if final:
        acc = _rms(acc, vec_ref[0, 1:2, :])
    o_ref[...] = acc


def _ffn_call(layer, x2d, wts, TM, final):
    M = x2d.shape[0]
    return pl.pallas_call(
        functools.partial(_ffn_kernel, final),
        grid=(M // TM,),
        in_specs=[
            pl.BlockSpec((TM, D_MODEL), lambda i: (i, 0)),
            pl.BlockSpec((1, 8, D_MODEL), lambda i: (layer, 0, 0)),
            pl.BlockSpec((1, D_MODEL, 2 * D_FF), lambda i: (layer, 0, 0)),
            pl.BlockSpec((1, D_FF, D_MODEL), lambda i: (layer, 0, 0)),
        ],
        out_specs=pl.BlockSpec((TM, D_MODEL), lambda i: (i, 0)),
        out_shape=jax.ShapeDtypeStruct(x2d.shape, F32),
        compiler_params=pltpu.CompilerParams(
            dimension_semantics=("arbitrary",),
            vmem_limit_bytes=VMEM_LIMIT_FFN),
        name="ffn",
    )(x2d, wts['ffn_vec'], wts['w_ffn_in'], wts['w_ffn_out'])


def _block_diag(w):
    P, n, d, e = w.shape
    out = jnp.zeros((P, n * d, n * e), w.dtype)
    for i in range(n):
        out = out.at[:, i * d:(i + 1) * d, i * e:(i + 1) * e].set(w[:, i])
    return out


def _pack_rows(P, rows):
    table = jnp.zeros((P, N_VEC, VEC_W), F32)
    for r, arr in rows:
        arr = arr.reshape(P, -1, arr.shape[-1]).astype(F32)
        table = table.at[:, r:r + arr.shape[1], 0:arr.shape[2]].set(arr)
    return table


def _run_stream(x, states, wts, TT, L, TM):
    B, T, _ = x.shape
    new_states = []
    for l in range(DEPTH):
        st_l = tuple(s[l] for s in states)
        x, st_new = _mixer_call(l, x, st_l, wts, TT, L)
        new_states.append(st_new)
        x = _ffn_call(l, x.reshape(B * T, D_MODEL), wts, TM, l == DEPTH - 1).reshape(B, T, D_MODEL)
    stacked = tuple(jnp.stack([st[i] for st in new_states]) for i in range(len(states)))
    return x, stacked


def kernel(x_prompt, x_sample, state_conv_a, state_lru, state_shift_b, state_wkv, state_conv_c, state_mem_c, state_mem_n, state_mem_m, norm1, w_in, conv_a_w, conv_a_b, lru_wr, lru_br, lru_wi, lru_bi, lru_lambda, norm_a, rwkv_mu, rwkv_w0, rwkv_w2, rwkv_a0, rwkv_a2, rwkv_g2, rwkv_kk, rwkv_ka, rwkv_rk, rwkv_lnw, rwkv_lnb, conv_c_w, conv_c_b, mlstm_wq, mlstm_wk, mlstm_wif, mlstm_bif, mlstm_gn, w_out, norm2, w_ffn_in, w_ffn_out, norm_f):
    P = DEPTH
    wif_pad = (jnp.zeros((P, 3 * D_C, 2 * GATE_W), F32)
               .at[:, :, 0:N_HEAD_C].set(mlstm_wif[:, :, 0:N_HEAD_C])
               .at[:, :, GATE_W:GATE_W + N_HEAD_C].set(mlstm_wif[:, :, N_HEAD_C:]))
    wts = {
        'vec': _pack_rows(P, [
            (V_NORM1, norm1), (V_CONV_A_B, conv_a_b),
            (V_LRU_B, jnp.concatenate([lru_br, lru_bi], axis=-1)), (V_LAM, lru_lambda),
            (V_NORM_A, norm_a), (V_MU, rwkv_mu), (V_W0, rwkv_w0), (V_A0, rwkv_a0),
            (V_KK, rwkv_kk), (V_KA, rwkv_ka), (V_RK, rwkv_rk.reshape(P, D_B)),
            (V_LNW, rwkv_lnw), (V_LNB, rwkv_lnb), (V_CONV_C_B, conv_c_b),
            (V_BIF, mlstm_bif[:, 0:N_HEAD_C]), (V_BIF2, mlstm_bif[:, N_HEAD_C:]), (V_GN, mlstm_gn),
            (V_CONV_A_W, conv_a_w), (V_CONV_C_W, conv_c_w)]),
        'w_in': w_in.astype(BF16),
        'lruw': jnp.concatenate([_block_diag(lru_wr), _block_diag(lru_wi)], axis=-1).astype(BF16),
        'w2': rwkv_w2.astype(BF16),
        'a2': rwkv_a2.astype(BF16),
        'g2': rwkv_g2.astype(BF16),
        'wqk': jnp.concatenate([_block_diag(mlstm_wq), _block_diag(mlstm_wk)], axis=-1).astype(BF16),
        'wif': wif_pad.astype(BF16),
        'w_out': w_out.astype(BF16),
        'ffn_vec': jnp.zeros((P, 8, D_MODEL), F32).at[:, 0].set(norm2).at[:, 1].set(
            jnp.broadcast_to(norm_f, (P, D_MODEL))),
        'w_ffn_in': w_ffn_in.astype(BF16),
        'w_ffn_out': w_ffn_out.astype(BF16),
    }

    def to_kernel_layout(conv_a, lru, shift, wkv, conv_c, mem_c, mem_n, mem_m):
        Bn = lru.shape[1]
        return (conv_a, lru.reshape(P, Bn, 1, D_A), shift, jnp.swapaxes(wkv, -1, -2), conv_c,
                mem_c, mem_n, mem_m.reshape(P, Bn, 1, N_HEAD_C))

    def from_kernel_layout(conv_a, lru, shift, wkv, conv_c, mem_c, mem_n, mem_m):
        Bn = lru.shape[1]
        return (conv_a, lru.reshape(P, Bn, D_A), shift, jnp.swapaxes(wkv, -1, -2), conv_c,
                mem_c, mem_n, mem_m.reshape(P, Bn, N_HEAD_C))

    Bp, Tp, _ = x_prompt.shape
    Bs, Ts, _ = x_sample.shape
    zero_states = (jnp.zeros((P, Bp, CONV_W - 1, D_A), F32),
                   jnp.zeros((P, Bp, D_A), F32),
                   jnp.zeros((P, Bp, 1, D_B_IN), F32),
                   jnp.zeros((P, Bp, N_HEAD_B, HEAD_DIM, HEAD_DIM), F32),
                   jnp.zeros((P, Bp, CONV_W - 1, D_C), F32),
                   jnp.zeros((P, Bp, N_HEAD_C, HEAD_DIM, HEAD_DIM), F32),
                   jnp.zeros((P, Bp, N_HEAD_C, HEAD_DIM), F32),
                   jnp.zeros((P, Bp, N_HEAD_C), F32))
    Lp = MLSTM_CHUNK if Tp % MLSTM_CHUNK == 0 else Tp
    Ls = MLSTM_CHUNK if Ts % MLSTM_CHUNK == 0 else Ts
    TTp = 256 if Tp % 256 == 0 else Lp
    y_prompt, p_states = _run_stream(x_prompt, to_kernel_layout(*zero_states), wts, TTp, Lp, 256)
    y_sample, s_states = _run_stream(
        x_sample, to_kernel_layout(state_conv_a, state_lru, state_shift_b, state_wkv,
                                   state_conv_c, state_mem_c, state_mem_n, state_mem_m),
        wts, Ls, Ls, Bs * Ts)
    return (y_prompt, y_sample) + from_kernel_layout(*p_states) + from_kernel_layout(*s_states)
```

```python
import functools
import math

import jax
import jax.numpy as jnp
from jax import lax
from jax.experimental import pallas as pl
from jax.experimental.pallas import tpu as pltpu

F32 = jnp.float32
BF16 = jnp.bfloat16

D_MODEL = 1024
DEPTH = 4
HEAD_DIM = 64
D_A = 256
N_BLK_A = 4
CONV_W = 4
LRU_C = 8.0
D_B = 384
N_HEAD_B = 6
LORA_W = 64
LORA_A = 64
LORA_G = 128
D_B_IN = 3 * D_B + LORA_W + LORA_A + LORA_G
D_C = 384
N_HEAD_C = 6
D_IN = 2 * D_A + D_B_IN + 3 * D_C
D_FF = 2816
RMS_EPS = 1e-6
GN_EPS_B = 64e-5
GN_EPS_C = 1e-6
MLSTM_CHUNK = 64

OFF_PB = 2 * D_A
OFF_PC = OFF_PB + D_B_IN
HIST = 8
GATE_W = 128
PAIR_W = 2 * HEAD_DIM
N_PAIR_C = N_HEAD_C // 2

(V_NORM1, V_CONV_A_B, V_LRU_B, V_LAM, V_NORM_A, V_MU, V_W0, V_A0, V_KK, V_KA, V_RK,
 V_LNW, V_LNB, V_CONV_C_B, V_BIF, V_GN, V_CONV_A_W, V_CONV_C_W, V_BIF2) = (
     0, 1, 2, 3, 4, 5, 6, 7, 8, 9, 10, 11, 12, 13, 14, 15, 16, 20, 24)
N_VEC = 32
VEC_W = D_B_IN

VMEM_LIMIT_MIXER = 48 * 1024 * 1024
VMEM_LIMIT_FFN = 56 * 1024 * 1024


def _bdot(a, b):
    return jnp.dot(a.astype(BF16), b.astype(BF16), preferred_element_type=F32)


def _bdot_nt(a, b):
    return lax.dot_general(a.astype(BF16), b.astype(BF16), (((1,), (1,)), ((), ())),
                           preferred_element_type=F32)


def _split(a, n):
    parts = []
    rem = a
    for i in range(n):
        p = rem.astype(BF16)
        parts.append(p)
        if i + 1 < n:
            rem = rem - p.astype(F32)
    return parts


def _edot_r(a, e, n=3):
    acc = None
    for p in _split(a, n):
        t = jnp.dot(p, e, preferred_element_type=F32)
        acc = t if acc is None else acc + t
    return acc


def _edot_l(e, a, n=3):
    acc = None
    for p in _split(a, n):
        t = jnp.dot(e, p, preferred_element_type=F32)
        acc = t if acc is None else acc + t
    return acc


def _xdot(a, b, nt=False):
    return _bdot_nt(a, b) if nt else _bdot(a, b)


def _softplus(z):
    return jnp.maximum(z, 0.0) + jnp.log(1.0 + jnp.exp(-jnp.abs(z)))


def _sigmoid(z):
    return jax.nn.sigmoid(z)


def _rms(x, g):
    return x * lax.rsqrt(jnp.mean(x * x, axis=-1, keepdims=True) + RMS_EPS) * g


def _iota2(shape, dim):
    return lax.broadcasted_iota(jnp.int32, shape, dim)


def _mixer_kernel(TT, L,
                  x_ref, ca_ref, lru_ref, sh_ref, wkv_ref, cc_ref, mc_ref, mn_ref, mm_ref,
                  vec_ref, w_in_ref, lruw_ref, w2_ref, a2_ref, g2_ref, wqk_ref, wif_ref, w_out_ref,
                  xo_ref, ca_o, lru_o, sh_o, wkv_o, cc_o, mc_o, mn_o, mm_o,
                  proj_s, ymix_s, rk_s, ml_s, yb_s, cbd_s, nrow_s, mrow_s):
    j = pl.program_id(1)
    NC = TT // L
    LOG_L = int(math.log2(L))
    CH = range(NC)
    HB = range(N_HEAD_B)
    sls = [slice(h * HEAD_DIM, (h + 1) * HEAD_DIM) for h in HB]
    rws = [slice(c * L, (c + 1) * L) for c in CH]

    def vec(row, width):
        return vec_ref[0, row:row + 1, 0:width]

    @pl.when(j == 0)
    def _init():
        proj_s[0:HIST, :] = jnp.zeros((HIST, D_IN), F32)
        proj_s[HIST - 3:HIST, 0:D_A] = ca_ref[0]
        proj_s[HIST - 1:HIST, OFF_PB:OFF_PB + D_B_IN] = sh_ref[0]
        proj_s[HIST - 3:HIST, OFF_PC:OFF_PC + D_C] = cc_ref[0]
        lru_o[...] = lru_ref[...]
        wkv_o[...] = wkv_ref[...]
        cbd_s[...] = jnp.zeros((N_PAIR_C, PAIR_W, PAIR_W), F32)
        mrow_s[...] = jnp.zeros((1, GATE_W), F32)
        for h in range(N_HEAD_C):
            cbd_s[h // 2, sls[h % 2], sls[h % 2]] = mc_ref[0, h]
            nrow_s[:, sls[h]] = mn_ref[0, h:h + 1, :]
        mrow_s[:, 0:N_HEAD_C] = mm_ref[0]

    x = x_ref[0]
    hn = _rms(x, vec(V_NORM1, D_MODEL))
    proj_s[HIST:HIST + TT, :] = jnp.dot(hn.astype(BF16), w_in_ref[0],
                                        preferred_element_type=F32)

    def taps(col0, width, w_row0):
        acc = None
        for t in range(CONV_W):
            r0 = HIST - (CONV_W - 1) + t
            term = vec(w_row0 + t, width) * proj_s[r0:r0 + TT, col0:col0 + width]
            acc = term if acc is None else acc + term
        return acc

    row_t = _iota2((TT, 1), 0)

    xa = taps(0, D_A, V_CONV_A_W) + vec(V_CONV_A_B, D_A)
    gates_a = _bdot(xa, lruw_ref[0]) + vec(V_LRU_B, 2 * D_A)
    gate_r = _sigmoid(gates_a[:, 0:D_A])
    gate_i = _sigmoid(gates_a[:, D_A:2 * D_A])
    log_a = (-LRU_C) * gate_r * _softplus(-vec(V_LAM, D_A))
    a_cum = jnp.exp(log_a)
    one_m_a2 = 1.0 - jnp.exp(2.0 * log_a)
    h_loc = jnp.where(one_m_a2 > 0.0, one_m_a2 * lax.rsqrt(one_m_a2), 0.0) * (gate_i * xa)
    d = 1
    while d < TT:
        keep = row_t >= d
        a_sh = jnp.where(keep, pltpu.roll(a_cum, d, 0), 1.0)
        h_sh = jnp.where(keep, pltpu.roll(h_loc, d, 0), 0.0)
        h_loc = a_cum * h_sh + h_loc
        a_cum = a_cum * a_sh
        d *= 2
    h_lru = h_loc + a_cum * lru_o[0]
    lru_o[0] = h_lru[TT - 1:TT, :]
    pag = proj_s[HIST:HIST + TT, D_A:2 * D_A]
    gelu = 0.5 * pag * (1.0 + jnp.tanh(math.sqrt(2.0 / math.pi)
                                       * (pag + 0.044715 * (pag * pag * pag))))
    ymix_s[:, 0:D_A] = _rms(h_lru, vec(V_NORM_A, D_A)) * gelu

    seg_r = _iota2((D_B, D_B), 0) >> 6
    seg_c = _iota2((D_B, D_B), 1) >> 6
    ones_bd = jnp.where(seg_r == seg_c, 1.0, 0.0).astype(BF16)
    seg_ones = jnp.where((_iota2((D_C, GATE_W), 0) >> 6) == _iota2((D_C, GATE_W), 1),
                         1.0, 0.0).astype(BF16)
    expand_m = jnp.where(_iota2((GATE_W, D_C), 0) == (_iota2((GATE_W, D_C), 1) >> 6),
                         1.0, 0.0).astype(BF16)
    tt_r = _iota2((TT, TT), 0)
    tt_c = _iota2((TT, TT), 1)
    same_chunk = (tt_r >> LOG_L) == (tt_c >> LOG_L)
    blk_tril = jnp.where(same_chunk & (tt_r >= tt_c), 1.0, 0.0).astype(BF16)

    def segsum(z):
        return _edot_r(z, ones_bd, n=2)

    def expand(z):
        return _edot_r(z, expand_m, n=2)

    pb = proj_s[HIST:HIST + TT, OFF_PB:OFF_PB + D_B_IN]
    prev = proj_s[HIST - 1:HIST - 1 + TT, OFF_PB:OFF_PB + D_B_IN]
    xs = pb + (prev - pb) * vec(V_MU, D_B_IN)
    r_b = xs[:, 0:D_B]
    k_b = xs[:, D_B:2 * D_B]
    v_b = xs[:, 2 * D_B:3 * D_B]
    o3 = 3 * D_B
    wd = xs[:, o3:o3 + LORA_W]
    ad = xs[:, o3 + LORA_W:o3 + LORA_W + LORA_A]
    gd = xs[:, o3 + LORA_W + LORA_A:D_B_IN]
    w_log = -_softplus(-(vec(V_W0, D_B) + _bdot(jnp.tanh(wd), w2_ref[0]))) - 0.5
    logw = -jnp.exp(w_log)
    a_b = _sigmoid(vec(V_A0, D_B) + _bdot(ad, a2_ref[0]))
    g_b = _bdot(_sigmoid(gd), g2_ref[0])
    kk = k_b * vec(V_KK, D_B)
    kk = kk * lax.rsqrt(jnp.maximum(segsum(kk * kk), 1e-24))
    k_b = k_b * (1.0 + (a_b - 1.0) * vec(V_KA, D_B))
    bonus = segsum(r_b * k_b * vec(V_RK, D_B)) * v_b
    ka_b = kk * a_b
    cl = _edot_l(blk_tril, logw)
    cl_last = [cl[c * L + L - 1:c * L + L, :] for c in CH]
    ctot = jnp.concatenate([jnp.broadcast_to(cl_last[c], (L, D_B)) for c in CH], axis=0)
    w_inv = jnp.exp(-cl)
    to_end = jnp.exp(ctot - cl)
    rk_s[0] = kk * jnp.exp(cl - logw)
    rk_s[1] = k_b * w_inv
    rk_s[2] = ka_b * w_inv
    rk_s[3] = r_b * jnp.exp(cl)
    rk_s[4] = v_b
    rk_s[5] = k_b * to_end
    rk_s[6] = ka_b * to_end
    w_end = [jnp.exp(cl_last[c]) for c in CH]

    xconv = taps(OFF_PC, D_C, V_CONV_C_W) + vec(V_CONV_C_B, D_C)
    xact = xconv * _sigmoid(xconv)
    qk = _bdot(xact, wqk_ref[0])
    q_c = qk[:, 0:D_C]
    k_c = qk[:, D_C:2 * D_C]
    vc = proj_s[HIST:HIST + TT, OFF_PC + D_C:OFF_PC + 2 * D_C]
    gates_c = (_bdot(q_c, wif_ref[0, 0:D_C, :]) + _bdot(k_c, wif_ref[0, D_C:2 * D_C, :])
               + _bdot(vc, wif_ref[0, 2 * D_C:3 * D_C, :]))
    i_pre = gates_c[:, 0:GATE_W] + vec(V_BIF, GATE_W)
    logf = -_softplus(-(gates_c[:, GATE_W:2 * GATE_W] + vec(V_BIF2, GATE_W)))
    ml_s[0] = q_c
    ml_s[1] = k_c * (HEAD_DIM ** -0.5)

    last3 = proj_s[HIST + TT - 3:HIST + TT, :]
    proj_s[HIST - 3:HIST, :] = last3
    ca_o[0] = last3[:, 0:D_A]
    sh_o[0] = last3[2:3, OFF_PB:OFF_PB + D_B_IN]
    cc_o[0] = last3[:, OFF_PC:OFF_PC + D_C]

    e64 = _iota2((HEAD_DIM, HEAD_DIM), 0) == _iota2((HEAD_DIM, HEAD_DIM), 1)
    GROUPS = ((0, 4), (4, 2))

    def group_consts(nh):
        nl, w = nh * L, nh * HEAD_DIM
        t_l = _iota2((L, nl), 0)
        s_l = _iota2((L, nl), 1) & (L - 1)
        return dict(
            nat=jnp.where((_iota2((nl, w), 0) >> LOG_L) == (_iota2((nl, w), 1) >> 6),
                          1.0, 0.0).astype(BF16),
            sq=jnp.where((_iota2((nl, nl), 0) >> LOG_L) == (_iota2((nl, nl), 1) >> LOG_L),
                         1.0, 0.0).astype(BF16),
            t=t_l, s=s_l, strict=t_l > s_l, incl=t_l >= s_l,
            eye=jnp.where(t_l == s_l, 1.0, 0.0))

    gconst = {nh: group_consts(nh) for _, nh in GROUPS}

    def bdiag(z, mask):
        zb = z.astype(BF16)
        return jnp.concatenate([zb] * (mask.shape[0] // L), axis=0) * mask

    def mm(a_, b_bf16):
        return jnp.dot(a_.astype(BF16), b_bf16, preferred_element_type=F32)

    def mm_nt(a_, b_bf16):
        return lax.dot_general(a_.astype(BF16), b_bf16, (((1,), (1,)), ((), ())),
                               preferred_element_type=F32)

    CG = [(c, g) for c in CH for g in range(len(GROUPS))]

    def gslice(i, c, g):
        h0, nh = GROUPS[g]
        return rk_s[i, rws[c], h0 * HEAD_DIM:(h0 + nh) * HEAD_DIM]

    def gc(g):
        return gconst[GROUPS[g][1]]

    qg = {p: gslice(0, *p) for p in CG}
    rg = {p: gslice(3, *p) for p in CG}
    vg = {p: gslice(4, *p) for p in CG}
    qr = {p: jnp.concatenate([qg[p], rg[p]], axis=0) for p in CG}
    sa = {(c, g): mm_nt(qr[c, g], bdiag(gslice(2, c, g), gc(g)['nat'])) for c, g in CG}
    sk = {(c, g): mm_nt(qr[c, g], bdiag(gslice(1, c, g), gc(g)['nat'])) for c, g in CG}
    n_m = {(c, g): jnp.where(gc(g)['strict'], sa[c, g][0:L], 0.0) for c, g in CG}
    mra_m = {(c, g): jnp.where(gc(g)['incl'], sa[c, g][L:2 * L], 0.0) for c, g in CG}
    mk2 = {(c, g): jnp.concatenate([jnp.where(gc(g)['strict'], sk[c, g][0:L], 0.0),
                                    jnp.where(gc(g)['incl'], sk[c, g][L:2 * L], 0.0)], axis=0)
           for c, g in CG}
    mv2 = {(c, g): mm(mk2[c, g], bdiag(vg[c, g], gc(g)['nat'])) for c, g in CG}
    mkv = {p: mv2[p][0:L] for p in CG}
    mrkv = {p: mv2[p][L:2 * L] for p in CG}
    xinv = {(c, g): gc(g)['eye'] - jnp.where(((gc(g)['t'] & 1) == 1) & (gc(g)['s'] == gc(g)['t'] - 1),
                                              n_m[c, g], 0.0) for c, g in CG}
    for lg in range(1, LOG_L):
        lvl = {nh: (((k['t'] >> lg) & 1) == 1) & ((k['s'] >> lg) == (k['t'] >> lg) - 1)
               for nh, k in gconst.items()}
        xc = {(c, g): mm(xinv[c, g], bdiag(jnp.where(lvl[GROUPS[g][1]], n_m[c, g], 0.0), gc(g)['sq']))
              for c, g in CG}
        xinv = {(c, g): xinv[c, g] - mm(xc[c, g], bdiag(xinv[c, g], gc(g)['sq'])) for c, g in CG}
    p1 = {(c, g): mm(xinv[c, g], bdiag(qg[c, g], gc(g)['nat'])) for c, g in CG}
    p2 = {(c, g): mm(xinv[c, g], bdiag(mkv[c, g], gc(g)['nat'])) for c, g in CG}
    rq = {(c, g): rg[c, g] - mm(mra_m[c, g], bdiag(p1[c, g], gc(g)['nat'])) for c, g in CG}
    yc = {(c, g): mrkv[c, g] - mm(mra_m[c, g], bdiag(p2[c, g], gc(g)['nat'])) for c, g in CG}

    CHH = [(c, h) for c in CH for h in HB]

    def head_of(d, c, h):
        g = 0 if h < GROUPS[1][0] else 1
        i = h - GROUPS[g][0]
        return d[c, g][:, i * HEAD_DIM:(i + 1) * HEAD_DIM]

    khat_t = [rk_s[5, rws[c], :].T for c in CH]
    ahat_t = [rk_s[6, rws[c], :].T for c in CH]
    g_m = {(c, h): _xdot(ahat_t[c][sls[h], :], head_of(p1, c, h)) for c, h in CHH}
    h_m = {(c, h): _xdot(khat_t[c][sls[h], :], head_of(vg, c, h))
           - _xdot(ahat_t[c][sls[h], :], head_of(p2, c, h)) for c, h in CHH}
    w_end_col = {(c, h): jnp.sum(jnp.where(e64, w_end[c][:, sls[h]], 0.0), axis=1, keepdims=True)
                 for c, h in CHH}
    s_t = [wkv_o[0, h] for h in HB]
    for c in CH:
        y_h = [_xdot(head_of(rq, c, h), s_t[h]) + head_of(yc, c, h) for h in HB]
        s_t = [w_end_col[c, h] * s_t[h] - _xdot(g_m[c, h], s_t[h]) + h_m[c, h] for h in HB]
        for h in HB:
            yb_s[rws[c], sls[h]] = y_h[h]
    for h in HB:
        wkv_o[0, h] = s_t[h]

    y_b = yb_s[...]
    y_cen = y_b - segsum(y_b) * (1.0 / HEAD_DIM)
    y_var = segsum(y_cen * y_cen) * (1.0 / HEAD_DIM)
    y_n = y_cen * lax.rsqrt(y_var + GN_EPS_B) * vec(V_LNW, D_B) + vec(V_LNB, D_B)
    ymix_s[:, D_A:D_A + D_B] = (y_n + bonus) * g_b

    b_col = _edot_l(blk_tril, logf)
    g_col = i_pre - b_col
    g_max = g_col
    pos = row_t & (L - 1)
    d = 1
    while d < L:
        g_max = jnp.where(pos >= d, jnp.maximum(g_max, pltpu.roll(g_max, d, 0)), g_max)
        d *= 2
    m_loc = b_col + g_max
    g_last = [g_max[c * L + L - 1:c * L + L, :] for c in CH]
    b_end = [b_col[c * L + L - 1:c * L + L, :] for c in CH]
    m_loc_end = [b_end[c] + g_last[c] for c in CH]
    w_end_loc = jnp.concatenate([jnp.exp(g_col[rws[c], :] - g_last[c]) for c in CH], axis=0)
    kw = ml_s[1] * expand(w_end_loc)
    s_of_lane = _iota2((L, N_HEAD_C * L), 1) & (L - 1)
    t_of_row = _iota2((L, N_HEAD_C * L), 0)
    causal_t = t_of_row >= s_of_lane
    eye_t = t_of_row == s_of_lane
    bd_l = (_iota2((N_HEAD_C * L, D_C), 0) >> LOG_L) == (_iota2((N_HEAD_C * L, D_C), 1) >> 6)
    seg_ones_l = jnp.where((_iota2((N_HEAD_C * L, GATE_W), 0) >> LOG_L)
                           == _iota2((N_HEAD_C * L, GATE_W), 1), 1.0, 0.0).astype(BF16)
    expand_l = jnp.where(_iota2((GATE_W, N_HEAD_C * L), 0)
                         == (_iota2((GATE_W, N_HEAD_C * L), 1) >> LOG_L), 1.0, 0.0).astype(BF16)
    g_exp_l = _edot_r(g_col, expand_l)
    g_max_exp_l = _edot_r(g_max, expand_l)

    pws = [slice(p * PAIR_W, (p + 1) * PAIR_W) for p in range(N_PAIR_C)]
    pair_mask = (_iota2((PAIR_W, PAIR_W), 0) >> 6) == (_iota2((PAIR_W, PAIR_W), 1) >> 6)
    p_loc, den_loc, c_loc, n_loc = [], [], [], []
    for c in CH:
        q_cc = ml_s[0, rws[c], :]
        k_cc = ml_s[1, rws[c], :]
        v_cc = proj_s[HIST + c * L:HIST + (c + 1) * L, OFF_PC + D_C:OFF_PC + 2 * D_C]
        k_bd = jnp.where(bd_l, jnp.concatenate([k_cc] * N_HEAD_C, axis=0), 0.0)
        v_bd = jnp.where(bd_l, jnp.concatenate([v_cc] * N_HEAD_C, axis=0), 0.0)
        g_flat = jnp.sum(jnp.where(eye_t, g_exp_l[rws[c], :], 0.0), axis=0, keepdims=True)
        dmat = jnp.where(causal_t, jnp.exp(jnp.minimum(g_flat - g_max_exp_l[rws[c], :], 0.0)), 0.0)
        scores = _bdot_nt(q_cc, k_bd) * dmat
        p_loc.append(_bdot(scores, v_bd))
        den_loc.append(_edot_r(scores, seg_ones_l, n=2))
        kw_c = kw[rws[c], :]
        kw_t = kw_c.T
        c_loc.append([jnp.where(pair_mask, _bdot(kw_t[pws[p], :], v_cc[:, pws[p]]), 0.0)
                      for p in range(N_PAIR_C)])
        n_loc.append(jnp.sum(kw_c, axis=0, keepdims=True))

    m_in = []
    m_cur = mrow_s[...]
    for c in CH:
        m_in.append(m_cur)
        m_cur = jnp.maximum(b_end[c] + m_cur, m_loc_end[c])
    m_out = m_in[1:] + [m_cur]
    mrow_s[...] = m_cur
    scale_rows = jnp.concatenate(
        [jnp.exp(b_end[c] + m_in[c] - m_out[c]) for c in CH]
        + [jnp.exp(m_loc_end[c] - m_out[c]) for c in CH], axis=0)
    scale_exp = expand(scale_rows)
    c_in, n_in = [], []
    c_cur = [cbd_s[p] for p in range(N_PAIR_C)]
    n_cur = nrow_s[...]
    for c in CH:
        c_in.append(c_cur)
        n_in.append(n_cur)
        w0 = scale_exp[c:c + 1, :]
        wl = scale_exp[NC + c:NC + c + 1, :]
        c_cur = [c_cur[p] * w0[:, pws[p]] + c_loc[c][p] * wl[:, pws[p]] for p in range(N_PAIR_C)]
        n_cur = n_cur * w0 + n_loc[c] * wl
    for p in range(N_PAIR_C):
        cbd_s[p] = c_cur[p]
    nrow_s[...] = n_cur
    for h in range(N_HEAD_C):
        mc_o[0, h] = c_cur[h // 2][sls[h % 2], sls[h % 2]]
        mn_o[0, h:h + 1, :] = n_cur[:, sls[h]]
    mm_o[0] = m_cur[:, 0:N_HEAD_C]

    m0_b = jnp.concatenate([jnp.broadcast_to(m_in[c], (L, GATE_W)) for c in CH], axis=0)
    log_inter = b_col + m0_b
    m_t = jnp.maximum(log_inter, m_loc)
    w_inter = jnp.exp(log_inter - m_t)
    w_local = jnp.exp(m_loc - m_t)
    q_all = ml_s[0]
    qc0 = jnp.concatenate(
        [jnp.concatenate([_bdot(q_all[rws[c], pws[p]], c_in[c][p]) for p in range(N_PAIR_C)], axis=1)
         for c in CH], axis=0)
    qn0 = jnp.concatenate([_edot_r(q_all[rws[c], :] * n_in[c], seg_ones) for c in CH], axis=0)
    den = w_inter * qn0 + w_local * jnp.concatenate(den_loc, axis=0)
    inv_den = 1.0 / jnp.maximum(jnp.abs(den), jnp.exp(-m_t))
    h_c = (expand(w_inter * inv_den) * qc0
           + expand(w_local * inv_den) * jnp.concatenate(p_loc, axis=0))

    h_cen = h_c - segsum(h_c) * (1.0 / HEAD_DIM)
    h_var = segsum(h_cen * h_cen) * (1.0 / HEAD_DIM)
    zc = proj_s[HIST:HIST + TT, OFF_PC + 2 * D_C:OFF_PC + 3 * D_C]
    ymix_s[:, D_A + D_B:D_MODEL] = (_sigmoid(zc) * (h_cen * lax.rsqrt(h_var + GN_EPS_C)
                                                   * vec(V_GN, D_C)))

    xo_ref[0] = x_ref[0] + jnp.dot(ymix_s[...].astype(BF16), w_out_ref[0],
                                   preferred_element_type=F32)


def _mixer_call(layer, x, states, wts, TT, L):
    B, T, _ = x.shape
    NT = T // TT

    def state_spec(shape):
        nd = len(shape)
        return pl.BlockSpec((1,) + tuple(shape[1:]), lambda b, j: (b,) + (0,) * (nd - 1))

    def layer_spec(arr):
        nd = arr.ndim
        return pl.BlockSpec((1,) + tuple(arr.shape[1:]), lambda b, j: (layer,) + (0,) * (nd - 1))

    x_spec = pl.BlockSpec((1, TT, D_MODEL), lambda b, j: (b, j, 0))
    state_specs = [state_spec(s.shape) for s in states]
    w_names = ('vec', 'w_in', 'lruw', 'w2', 'a2', 'g2', 'wqk', 'wif', 'w_out')
    w_arrs = [wts[n] for n in w_names]

    out_shape = ([jax.ShapeDtypeStruct(x.shape, F32)]
                 + [jax.ShapeDtypeStruct(s.shape, F32) for s in states])
    scratch = [
        pltpu.VMEM((HIST + TT, D_IN), F32),
        pltpu.VMEM((TT, D_MODEL), F32),
        pltpu.VMEM((7, TT, D_B), F32),
        pltpu.VMEM((2, TT, D_C), F32),
        pltpu.VMEM((TT, D_B), F32),
        pltpu.VMEM((N_PAIR_C, PAIR_W, PAIR_W), F32),
        pltpu.VMEM((1, D_C), F32),
        pltpu.VMEM((1, GATE_W), F32),
    ]
    outs = pl.pallas_call(
        functools.partial(_mixer_kernel, TT, L),
        grid=(B, NT),
        in_specs=[x_spec] + state_specs + [layer_spec(a) for a in w_arrs],
        out_specs=[x_spec] + state_specs,
        out_shape=out_shape,
        scratch_shapes=scratch,
        compiler_params=pltpu.CompilerParams(
            dimension_semantics=("arbitrary", "arbitrary"),
            vmem_limit_bytes=VMEM_LIMIT_MIXER),
        name="mixer",
    )(x, *states, *w_arrs)
    return outs[0], tuple(outs[1:])


FF_SPLIT = 2
FF_CHUNK = D_FF // FF_SPLIT


def _ffn_kernel(final, x_ref, vec_ref, wi_ref, wo_ref, o_ref):
    x = x_ref[...]
    hn = _rms(x, vec_ref[0, 0:1, :]).astype(BF16)
    acc = x
    for f in range(FF_SPLIT):
        lo = f * FF_CHUNK
        gate = jnp.dot(hn, wi_ref[0, :, lo:lo + FF_CHUNK], preferred_element_type=F32)
        up = jnp.dot(hn, wi_ref[0, :, D_FF + lo:D_FF + lo + FF_CHUNK],
                     preferred_element_type=F32)
        act = (gate * _sigmoid(gate) * up).astype(BF16)
        acc = acc + jnp.dot(act, wo_ref[0, lo:lo + FF_CHUNK, :], preferred_element_type=F32)
    if final:
        acc = _rms(acc, vec_ref[0, 1:2, :])
    o_ref[...] = acc


def _ffn_call(layer, x2d, wts, TM, final):
    M = x2d.shape[0]
    return pl.pallas_call(
        functools.partial(_ffn_kernel, final),
        grid=(M // TM,),
        in_specs=[
            pl.BlockSpec((TM, D_MODEL), lambda i: (i, 0)),
            pl.BlockSpec((1, 8, D_MODEL), lambda i: (layer, 0, 0)),
            pl.BlockSpec((1, D_MODEL, 2 * D_FF), lambda i: (layer, 0, 0)),
            pl.BlockSpec((1, D_FF, D_MODEL), lambda i: (layer, 0, 0)),
        ],
        out_specs=pl.BlockSpec((TM, D_MODEL), lambda i: (i, 0)),
        out_shape=jax.ShapeDtypeStruct(x2d.shape, F32),
        compiler_params=pltpu.CompilerParams(
            dimension_semantics=("arbitrary",),
            vmem_limit_bytes=VMEM_LIMIT_FFN),
        name="ffn",
    )(x2d, wts['ffn_vec'], wts['w_ffn_in'], wts['w_ffn_out'])


def _block_diag(w):
    P, n, d, e = w.shape
    out = jnp.zeros((P, n * d, n * e), w.dtype)
    for i in range(n):
        out = out.at[:, i * d:(i + 1) * d, i * e:(i + 1) * e].set(w[:, i])
    return out


def _pack_rows(P, rows):
    table = jnp.zeros((P, N_VEC, VEC_W), F32)
    for r, arr in rows:
        arr = arr.reshape(P, -1, arr.shape[-1]).astype(F32)
        table = table.at[:, r:r + arr.shape[1], 0:arr.shape[2]].set(arr)
    return table


def _run_stream(x, states, wts, TT, L, TM):
    B, T, _ = x.shape
    new_states = []
    for l in range(DEPTH):
        st_l = tuple(s[l] for s in states)
        x, st_new = _mixer_call(l, x, st_l, wts, TT, L)
        new_states.append(st_new)
        x = _ffn_call(l, x.reshape(B * T, D_MODEL), wts, TM, l == DEPTH - 1).reshape(B, T, D_MODEL)
    stacked = tuple(jnp.stack([st[i] for st in new_states]) for i in range(len(states)))
    return x, stacked


def kernel(x_prompt, x_sample, state_conv_a, state_lru, state_shift_b, state_wkv, state_conv_c, state_mem_c, state_mem_n, state_mem_m, norm1, w_in, conv_a_w, conv_a_b, lru_wr, lru_br, lru_wi, lru_bi, lru_lambda, norm_a, rwkv_mu, rwkv_w0, rwkv_w2, rwkv_a0, rwkv_a2, rwkv_g2, rwkv_kk, rwkv_ka, rwkv_rk, rwkv_lnw, rwkv_lnb, conv_c_w, conv_c_b, mlstm_wq, mlstm_wk, mlstm_wif, mlstm_bif, mlstm_gn, w_out, norm2, w_ffn_in, w_ffn_out, norm_f):
    P = DEPTH
    wif_pad = (jnp.zeros((P, 3 * D_C, 2 * GATE_W), F32)
               .at[:, :, 0:N_HEAD_C].set(mlstm_wif[:, :, 0:N_HEAD_C])
               .at[:, :, GATE_W:GATE_W + N_HEAD_C].set(mlstm_wif[:, :, N_HEAD_C:]))
    wts = {
        'vec': _pack_rows(P, [
            (V_NORM1, norm1), (V_CONV_A_B, conv_a_b),
            (V_LRU_B, jnp.concatenate([lru_br, lru_bi], axis=-1)), (V_LAM, lru_lambda),
            (V_NORM_A, norm_a), (V_MU, rwkv_mu), (V_W0, rwkv_w0), (V_A0, rwkv_a0),
            (V_KK, rwkv_kk), (V_KA, rwkv_ka), (V_RK, rwkv_rk.reshape(P, D_B)),
            (V_LNW, rwkv_lnw), (V_LNB, rwkv_lnb), (V_CONV_C_B, conv_c_b),
            (V_BIF, mlstm_bif[:, 0:N_HEAD_C]), (V_BIF2, mlstm_bif[:, N_HEAD_C:]), (V_GN, mlstm_gn),
            (V_CONV_A_W, conv_a_w), (V_CONV_C_W, conv_c_w)]),
        'w_in': w_in.astype(BF16),
        'lruw': jnp.concatenate([_block_diag(lru_wr), _block_diag(lru_wi)], axis=-1).astype(BF16),
        'w2': rwkv_w2.astype(BF16),
        'a2': rwkv_a2.astype(BF16),
        'g2': rwkv_g2.astype(BF16),
        'wqk': jnp.concatenate([_block_diag(mlstm_wq), _block_diag(mlstm_wk)], axis=-1).astype(BF16),
        'wif': wif_pad.astype(BF16),
        'w_out': w_out.astype(BF16),
        'ffn_vec': jnp.zeros((P, 8, D_MODEL), F32).at[:, 0].set(norm2).at[:, 1].set(
            jnp.broadcast_to(norm_f, (P, D_MODEL))),
        'w_ffn_in': w_ffn_in.astype(BF16),
        'w_ffn_out': w_ffn_out.astype(BF16),
    }

    def to_kernel_layout(conv_a, lru, shift, wkv, conv_c, mem_c, mem_n, mem_m):
        Bn = lru.shape[1]
        return (conv_a, lru.reshape(P, Bn, 1, D_A), shift, jnp.swapaxes(wkv, -1, -2), conv_c,
                mem_c, mem_n, mem_m.reshape(P, Bn, 1, N_HEAD_C))

    def from_kernel_layout(conv_a, lru, shift, wkv, conv_c, mem_c, mem_n, mem_m):
        Bn = lru.shape[1]
        return (conv_a, lru.reshape(P, Bn, D_A), shift, jnp.swapaxes(wkv, -1, -2), conv_c,
                mem_c, mem_n, mem_m.reshape(P, Bn, N_HEAD_C))

    Bp, Tp, _ = x_prompt.shape
    Bs, Ts, _ = x_sample.shape
    zero_states = (jnp.zeros((P, Bp, CONV_W - 1, D_A), F32),
                   jnp.zeros((P, Bp, D_A), F32),
                   jnp.zeros((P, Bp, 1, D_B_IN), F32),
                   jnp.zeros((P, Bp, N_HEAD_B, HEAD_DIM, HEAD_DIM), F32),
                   jnp.zeros((P, Bp, CONV_W - 1, D_C), F32),
                   jnp.zeros((P, Bp, N_HEAD_C, HEAD_DIM, HEAD_DIM), F32),
                   jnp.zeros((P, Bp, N_HEAD_C, HEAD_DIM), F32),
                   jnp.zeros((P, Bp, N_HEAD_C), F32))
    Lp = MLSTM_CHUNK if Tp % MLSTM_CHUNK == 0 else Tp
    Ls = MLSTM_CHUNK if Ts % MLSTM_CHUNK == 0 else Ts
    TTp = 256 if Tp % 256 == 0 else Lp
    y_prompt, p_states = _run_stream(x_prompt, to_kernel_layout(*zero_states), wts, TTp, Lp, 256)
    y_sample, s_states = _run_stream(
        x_sample, to_kernel_layout(state_conv_a, state_lru, state_shift_b, state_wkv,
                                   state_conv_c, state_mem_c, state_mem_n, state_mem_m),
        wts, Ls, Ls, Bs * Ts)
    return (y_prompt, y_sample) + from_kernel_layout(*p_states) + from_kernel_layout(*s_states)
```

```python
import functools
import math

import jax
import jax.numpy as jnp
from jax import lax
from jax.experimental import pallas as pl
from jax.experimental.pallas import tpu as pltpu

F32 = jnp.float32
BF16 = jnp.bfloat16

D_MODEL = 1024
DEPTH = 4
HEAD_DIM = 64
D_A = 256
N_BLK_A = 4
CONV_W = 4
LRU_C = 8.0
D_B = 384
N_HEAD_B = 6
LORA_W = 64
LORA_A = 64
LORA_G = 128
D_B_IN = 3 * D_B + LORA_W + LORA_A + LORA_G
D_C = 384
N_HEAD_C = 6
D_IN = 2 * D_A + D_B_IN + 3 * D_C
D_FF = 2816
RMS_EPS = 1e-6
GN_EPS_B = 64e-5
GN_EPS_C = 1e-6
MLSTM_CHUNK = 64

OFF_PB = 2 * D_A
OFF_PC = OFF_PB + D_B_IN
HIST = 8
GATE_W = 128
PAIR_W = 2 * HEAD_DIM
N_PAIR_C = N_HEAD_C // 2

(V_NORM1, V_CONV_A_B, V_LRU_B, V_LAM, V_NORM_A, V_MU, V_W0, V_A0, V_KK, V_KA, V_RK,
 V_LNW, V_LNB, V_CONV_C_B, V_BIF, V_GN, V_CONV_A_W, V_CONV_C_W, V_BIF2) = (
     0, 1, 2, 3, 4, 5, 6, 7, 8, 9, 10, 11, 12, 13, 14, 15, 16, 20, 24)
N_VEC = 32
VEC_W = D_B_IN

VMEM_LIMIT_MIXER = 48 * 1024 * 1024
VMEM_LIMIT_FFN = 56 * 1024 * 1024


def _bdot(a, b):
    return jnp.dot(a.astype(BF16), b.astype(BF16), preferred_element_type=F32)


def _bdot_nt(a, b):
    return lax.dot_general(a.astype(BF16), b.astype(BF16), (((1,), (1,)), ((), ())),
                           preferred_element_type=F32)


def _split(a, n):
    parts = []
    rem = a
    for i in range(n):
        p = rem.astype(BF16)
        parts.append(p)
        if i + 1 < n:
            rem = rem - p.astype(F32)
    return parts


def _edot_r(a, e, n=3):
    acc = None
    for p in _split(a, n):
        t = jnp.dot(p, e, preferred_element_type=F32)
        acc = t if acc is None else acc + t
    return acc


def _edot_l(e, a, n=3):
    acc = None
    for p in _split(a, n):
        t = jnp.dot(e, p, preferred_element_type=F32)
        acc = t if acc is None else acc + t
    return acc


def _xdot(a, b, nt=False):
    return _bdot_nt(a, b) if nt else _bdot(a, b)


def _softplus(z):
    return jnp.maximum(z, 0.0) + jnp.log(1.0 + jnp.exp(-jnp.abs(z)))


def _sigmoid(z):
    return 0.5 * jnp.tanh(0.5 * z) + 0.5


def _rms(x, g):
    return x * lax.rsqrt(jnp.mean(x * x, axis=-1, keepdims=True) + RMS_EPS) * g


def _iota2(shape, dim):
    return lax.broadcasted_iota(jnp.int32, shape, dim)


def _mixer_kernel(TT, L,
                  x_ref, ca_ref, lru_ref, sh_ref, wkv_ref, cc_ref, mc_ref, mn_ref, mm_ref,
                  vec_ref, w_in_ref, lruw_ref, w2_ref, a2_ref, g2_ref, wqk_ref, wif_ref, w_out_ref,
                  xo_ref, ca_o, lru_o, sh_o, wkv_o, cc_o, mc_o, mn_o, mm_o,
                  proj_s, rk_s, ml_s, yb_s, cbd_s, nrow_s, mrow_s):
    j = pl.program_id(1)
    NC = TT // L
    LOG_L = int(math.log2(L))
    CH = range(NC)
    HB = range(N_HEAD_B)
    sls = [slice(h * HEAD_DIM, (h + 1) * HEAD_DIM) for h in HB]
    rws = [slice(c * L, (c + 1) * L) for c in CH]

    def vec(row, width):
        return vec_ref[0, row:row + 1, 0:width]

    @pl.when(j == 0)
    def _init():
        proj_s[0:HIST, :] = jnp.zeros((HIST, D_IN), F32)
        proj_s[HIST - 3:HIST, 0:D_A] = ca_ref[0]
        proj_s[HIST - 1:HIST, OFF_PB:OFF_PB + D_B_IN] = sh_ref[0]
        proj_s[HIST - 3:HIST, OFF_PC:OFF_PC + D_C] = cc_ref[0]
        lru_o[...] = lru_ref[...]
        wkv_o[...] = wkv_ref[...]
        cbd_s[...] = jnp.zeros((N_PAIR_C, PAIR_W, PAIR_W), F32)
        mrow_s[...] = jnp.zeros((1, GATE_W), F32)
        for h in range(N_HEAD_C):
            cbd_s[h // 2, sls[h % 2], sls[h % 2]] = mc_ref[0, h]
            nrow_s[:, sls[h]] = mn_ref[0, h:h + 1, :]
        mrow_s[:, 0:N_HEAD_C] = mm_ref[0]

    x = x_ref[0]
    hn = _rms(x, vec(V_NORM1, D_MODEL))
    hn_b = hn.astype(BF16)
    for lo, hi in ((0, OFF_PB), (OFF_PB, OFF_PC), (OFF_PC, D_IN)):
        proj_s[HIST:HIST + TT, lo:hi] = jnp.dot(hn_b, w_in_ref[0, :, lo:hi],
                                                preferred_element_type=F32)
    out_parts = {}

    def taps(col0, width, w_row0):
        acc = None
        for t in range(CONV_W):
            r0 = HIST - (CONV_W - 1) + t
            term = vec(w_row0 + t, width) * proj_s[r0:r0 + TT, col0:col0 + width]
            acc = term if acc is None else acc + term
        return acc

    row_t = _iota2((TT, 1), 0)

    seg_r = _iota2((D_B, D_B), 0) >> 6
    seg_c = _iota2((D_B, D_B), 1) >> 6
    ones_bd = jnp.where(seg_r == seg_c, 1.0, 0.0).astype(BF16)
    seg_ones = jnp.where((_iota2((D_C, GATE_W), 0) >> 6) == _iota2((D_C, GATE_W), 1),
                         1.0, 0.0).astype(BF16)
    expand_m = jnp.where(_iota2((GATE_W, D_C), 0) == (_iota2((GATE_W, D_C), 1) >> 6),
                         1.0, 0.0).astype(BF16)
    tt_r = _iota2((TT, TT), 0)
    tt_c = _iota2((TT, TT), 1)
    same_chunk = (tt_r >> LOG_L) == (tt_c >> LOG_L)
    blk_tril = jnp.where(same_chunk & (tt_r >= tt_c), 1.0, 0.0).astype(BF16)

    def segsum(z):
        return _edot_r(z, ones_bd, n=2)

    def expand(z):
        return _edot_r(z, expand_m, n=2)

    def group_a():
        xa = taps(0, D_A, V_CONV_A_W) + vec(V_CONV_A_B, D_A)
        gates_a = _bdot(xa, lruw_ref[0]) + vec(V_LRU_B, 2 * D_A)
        gate_r = _sigmoid(gates_a[:, 0:D_A])
        gate_i = _sigmoid(gates_a[:, D_A:2 * D_A])
        log_a = (-LRU_C) * gate_r * _softplus(-vec(V_LAM, D_A))
        a_cum = jnp.exp(log_a)
        one_m_a2 = 1.0 - jnp.exp(2.0 * log_a)
        h_loc = jnp.where(one_m_a2 > 0.0, one_m_a2 * lax.rsqrt(one_m_a2), 0.0) * (gate_i * xa)
        yield
        d = 1
        while d < TT:
            keep = row_t >= d
            a_sh = jnp.where(keep, pltpu.roll(a_cum, d, 0), 1.0)
            h_sh = jnp.where(keep, pltpu.roll(h_loc, d, 0), 0.0)
            h_loc = a_cum * h_sh + h_loc
            a_cum = a_cum * a_sh
            d *= 2
            yield
        h_lru = h_loc + a_cum * lru_o[0]
        lru_o[0] = h_lru[TT - 1:TT, :]
        pag = proj_s[HIST:HIST + TT, D_A:2 * D_A]
        gelu = 0.5 * pag * (1.0 + jnp.tanh(math.sqrt(2.0 / math.pi)
                                           * (pag + 0.044715 * (pag * pag * pag))))
        ya = _rms(h_lru, vec(V_NORM_A, D_A)) * gelu
        yield
        out_parts['a'] = jnp.dot(ya.astype(BF16), w_out_ref[0, 0:D_A, :],
                                 preferred_element_type=F32)

    def group_b():
        pb = proj_s[HIST:HIST + TT, OFF_PB:OFF_PB + D_B_IN]
        prev = proj_s[HIST - 1:HIST - 1 + TT, OFF_PB:OFF_PB + D_B_IN]
        xs = pb + (prev - pb) * vec(V_MU, D_B_IN)
        r_b = xs[:, 0:D_B]
        k_b = xs[:, D_B:2 * D_B]
        v_b = xs[:, 2 * D_B:3 * D_B]
        o3 = 3 * D_B
        wd = xs[:, o3:o3 + LORA_W]
        ad = xs[:, o3 + LORA_W:o3 + LORA_W + LORA_A]
        gd = xs[:, o3 + LORA_W + LORA_A:D_B_IN]
        w_log = -_softplus(-(vec(V_W0, D_B) + _bdot(jnp.tanh(wd), w2_ref[0]))) - 0.5
        logw = -jnp.exp(w_log)
        a_b = _sigmoid(vec(V_A0, D_B) + _bdot(ad, a2_ref[0]))
        g_b = _bdot(_sigmoid(gd), g2_ref[0])
        yield
        kk = k_b * vec(V_KK, D_B)
        kk = kk * lax.rsqrt(jnp.maximum(segsum(kk * kk), 1e-24))
        k_b = k_b * (1.0 + (a_b - 1.0) * vec(V_KA, D_B))
        bonus = segsum(r_b * k_b * vec(V_RK, D_B)) * v_b
        ka_b = kk * a_b
        yield
        cl = _edot_l(blk_tril, logw)
        cl_last = [cl[c * L + L - 1:c * L + L, :] for c in CH]
        ctot = jnp.concatenate([jnp.broadcast_to(cl_last[c], (L, D_B)) for c in CH], axis=0)
        w_inv = jnp.exp(-cl)
        to_end = jnp.exp(ctot - cl)
        rk_s[0] = kk * jnp.exp(cl - logw)
        rk_s[1] = k_b * w_inv
        rk_s[2] = ka_b * w_inv
        rk_s[3] = r_b * jnp.exp(cl)
        rk_s[4] = v_b
        rk_s[5] = k_b * to_end
        rk_s[6] = ka_b * to_end
        w_end = [jnp.exp(cl_last[c]) for c in CH]

        yield
        e64 = _iota2((HEAD_DIM, HEAD_DIM), 0) == _iota2((HEAD_DIM, HEAD_DIM), 1)
        GROUPS = ((0, 4), (4, 2))

        def group_consts(nh):
            nl, w = nh * L, nh * HEAD_DIM
            t_l = _iota2((L, nl), 0)
            s_l = _iota2((L, nl), 1) & (L - 1)
            return dict(
                nat=jnp.where((_iota2((nl, w), 0) >> LOG_L) == (_iota2((nl, w), 1) >> 6),
                              1.0, 0.0).astype(BF16),
                sq=jnp.where((_iota2((nl, nl), 0) >> LOG_L) == (_iota2((nl, nl), 1) >> LOG_L),
                             1.0, 0.0).astype(BF16),
                t=t_l, s=s_l, strict=t_l > s_l, incl=t_l >= s_l,
                eye=jnp.where(t_l == s_l, 1.0, 0.0))

        gconst = {nh: group_consts(nh) for _, nh in GROUPS}

        def bdiag(z, mask):
            zb = z.astype(BF16)
            return jnp.concatenate([zb] * (mask.shape[0] // L), axis=0) * mask

        def mm(a_, b_bf16):
            return jnp.dot(a_.astype(BF16), b_bf16, preferred_element_type=F32)

        def mm_nt(a_, b_bf16):
            return lax.dot_general(a_.astype(BF16), b_bf16, (((1,), (1,)), ((), ())),
                                   preferred_element_type=F32)

        CG = [(c, g) for c in CH for g in range(len(GROUPS))]

        def gslice(i, c, g):
            h0, nh = GROUPS[g]
            return rk_s[i, rws[c], h0 * HEAD_DIM:(h0 + nh) * HEAD_DIM]

        def gc(g):
            return gconst[GROUPS[g][1]]

        qg = {p: gslice(0, *p) for p in CG}
        rg = {p: gslice(3, *p) for p in CG}
        vg = {p: gslice(4, *p) for p in CG}
        qr = {p: jnp.concatenate([qg[p], rg[p]], axis=0) for p in CG}
        sa = {(c, g): mm_nt(qr[c, g], bdiag(gslice(2, c, g), gc(g)['nat'])) for c, g in CG}
        sk = {(c, g): mm_nt(qr[c, g], bdiag(gslice(1, c, g), gc(g)['nat'])) for c, g in CG}
        n_m = {(c, g): jnp.where(gc(g)['strict'], sa[c, g][0:L], 0.0) for c, g in CG}
        mra_m = {(c, g): jnp.where(gc(g)['incl'], sa[c, g][L:2 * L], 0.0) for c, g in CG}
        mk2 = {(c, g): jnp.concatenate([jnp.where(gc(g)['strict'], sk[c, g][0:L], 0.0),
                                        jnp.where(gc(g)['incl'], sk[c, g][L:2 * L], 0.0)], axis=0)
               for c, g in CG}
        yield
        mv2 = {(c, g): mm(mk2[c, g], bdiag(vg[c, g], gc(g)['nat'])) for c, g in CG}
        mkv = {p: mv2[p][0:L] for p in CG}
        mrkv = {p: mv2[p][L:2 * L] for p in CG}
        yield
        xinv = {(c, g): gc(g)['eye'] - jnp.where(((gc(g)['t'] & 1) == 1) & (gc(g)['s'] == gc(g)['t'] - 1),
                                                  n_m[c, g], 0.0) for c, g in CG}
        for lg in range(1, LOG_L):
            lvl = {nh: (((k['t'] >> lg) & 1) == 1) & ((k['s'] >> lg) == (k['t'] >> lg) - 1)
                   for nh, k in gconst.items()}
            xc = {(c, g): mm(xinv[c, g], bdiag(jnp.where(lvl[GROUPS[g][1]], n_m[c, g], 0.0), gc(g)['sq']))
                  for c, g in CG}
            yield
            xinv = {(c, g): xinv[c, g] - mm(xc[c, g], bdiag(xinv[c, g], gc(g)['sq'])) for c, g in CG}
            yield
        p1 = {(c, g): mm(xinv[c, g], bdiag(qg[c, g], gc(g)['nat'])) for c, g in CG}
        p2 = {(c, g): mm(xinv[c, g], bdiag(mkv[c, g], gc(g)['nat'])) for c, g in CG}
        yield
        rq = {(c, g): rg[c, g] - mm(mra_m[c, g], bdiag(p1[c, g], gc(g)['nat'])) for c, g in CG}
        yc = {(c, g): mrkv[c, g] - mm(mra_m[c, g], bdiag(p2[c, g], gc(g)['nat'])) for c, g in CG}

        yield
        CHH = [(c, h) for c in CH for h in HB]

        def head_of(d, c, h):
            g = 0 if h < GROUPS[1][0] else 1
            i = h - GROUPS[g][0]
            return d[c, g][:, i * HEAD_DIM:(i + 1) * HEAD_DIM]

        khat_t = [rk_s[5, rws[c], :].T for c in CH]
        ahat_t = [rk_s[6, rws[c], :].T for c in CH]
        g_m = {(c, h): _xdot(ahat_t[c][sls[h], :], head_of(p1, c, h)) for c, h in CHH}
        h_m = {(c, h): _xdot(khat_t[c][sls[h], :], head_of(vg, c, h))
               - _xdot(ahat_t[c][sls[h], :], head_of(p2, c, h)) for c, h in CHH}
        w_end_col = {(c, h): jnp.sum(jnp.where(e64, w_end[c][:, sls[h]], 0.0), axis=1, keepdims=True)
                     for c, h in CHH}
        yield
        s_t = [wkv_o[0, h] for h in HB]
        for c in CH:
            y_h = [_xdot(head_of(rq, c, h), s_t[h]) + head_of(yc, c, h) for h in HB]
            s_t = [w_end_col[c, h] * s_t[h] - _xdot(g_m[c, h], s_t[h]) + h_m[c, h] for h in HB]
            for h in HB:
                yb_s[rws[c], sls[h]] = y_h[h]
            yield
        for h in HB:
            wkv_o[0, h] = s_t[h]

        y_b = yb_s[...]
        y_cen = y_b - segsum(y_b) * (1.0 / HEAD_DIM)
        yield
        y_var = segsum(y_cen * y_cen) * (1.0 / HEAD_DIM)
        y_n = y_cen * lax.rsqrt(y_var + GN_EPS_B) * vec(V_LNW, D_B) + vec(V_LNB, D_B)
        yb_out = (y_n + bonus) * g_b
        yield
        out_parts['b'] = jnp.dot(yb_out.astype(BF16), w_out_ref[0, D_A:D_A + D_B, :],
                                 preferred_element_type=F32)

    def group_c():
        xconv = taps(OFF_PC, D_C, V_CONV_C_W) + vec(V_CONV_C_B, D_C)
        xact = xconv * _sigmoid(xconv)
        qk = _bdot(xact, wqk_ref[0])
        q_c = qk[:, 0:D_C]
        k_c = qk[:, D_C:2 * D_C]
        vc = proj_s[HIST:HIST + TT, OFF_PC + D_C:OFF_PC + 2 * D_C]
        gates_c = (_bdot(q_c, wif_ref[0, 0:D_C, :]) + _bdot(k_c, wif_ref[0, D_C:2 * D_C, :])
                   + _bdot(vc, wif_ref[0, 2 * D_C:3 * D_C, :]))
        i_pre = gates_c[:, 0:GATE_W] + vec(V_BIF, GATE_W)
        logf = -_softplus(-(gates_c[:, GATE_W:2 * GATE_W] + vec(V_BIF2, GATE_W)))
        ml_s[0] = q_c
        ml_s[1] = k_c * (HEAD_DIM ** -0.5)

        yield
        b_col = _edot_l(blk_tril, logf)
        yield
        g_col = i_pre - b_col
        g_max = g_col
        pos = row_t & (L - 1)
        d = 1
        while d < L:
            g_max = jnp.where(pos >= d, jnp.maximum(g_max, pltpu.roll(g_max, d, 0)), g_max)
            d *= 2
        yield
        m_loc = b_col + g_max
        g_last = [g_max[c * L + L - 1:c * L + L, :] for c in CH]
        b_end = [b_col[c * L + L - 1:c * L + L, :] for c in CH]
        m_loc_end = [b_end[c] + g_last[c] for c in CH]
        w_end_loc = jnp.concatenate([jnp.exp(g_col[rws[c], :] - g_last[c]) for c in CH], axis=0)
        kw = ml_s[1] * expand(w_end_loc)
        yield
        s_of_lane = _iota2((L, N_HEAD_C * L), 1) & (L - 1)
        t_of_row = _iota2((L, N_HEAD_C * L), 0)
        causal_t = t_of_row >= s_of_lane
        eye_t = t_of_row == s_of_lane
        bd_l = (_iota2((N_HEAD_C * L, D_C), 0) >> LOG_L) == (_iota2((N_HEAD_C * L, D_C), 1) >> 6)
        seg_ones_l = jnp.where((_iota2((N_HEAD_C * L, GATE_W), 0) >> LOG_L)
                               == _iota2((N_HEAD_C * L, GATE_W), 1), 1.0, 0.0).astype(BF16)
        expand_l = jnp.where(_iota2((GATE_W, N_HEAD_C * L), 0)
                             == (_iota2((GATE_W, N_HEAD_C * L), 1) >> LOG_L), 1.0, 0.0).astype(BF16)
        g_exp_l = _edot_r(g_col, expand_l)
        g_max_exp_l = _edot_r(g_max, expand_l)

        pws = [slice(p * PAIR_W, (p + 1) * PAIR_W) for p in range(N_PAIR_C)]
        pair_mask = (_iota2((PAIR_W, PAIR_W), 0) >> 6) == (_iota2((PAIR_W, PAIR_W), 1) >> 6)
        yield
        p_loc, den_loc, c_loc, n_loc = [], [], [], []
        for c in CH:
            q_cc = ml_s[0, rws[c], :]
            k_cc = ml_s[1, rws[c], :]
            v_cc = proj_s[HIST + c * L:HIST + (c + 1) * L, OFF_PC + D_C:OFF_PC + 2 * D_C]
            k_bd = jnp.where(bd_l, jnp.concatenate([k_cc] * N_HEAD_C, axis=0), 0.0)
            v_bd = jnp.where(bd_l, jnp.concatenate([v_cc] * N_HEAD_C, axis=0), 0.0)
            g_flat = jnp.sum(jnp.where(eye_t, g_exp_l[rws[c], :], 0.0), axis=0, keepdims=True)
            dmat = jnp.where(causal_t, jnp.exp(jnp.minimum(g_flat - g_max_exp_l[rws[c], :], 0.0)), 0.0)
            scores = _bdot_nt(q_cc, k_bd) * dmat
            p_loc.append(_bdot(scores, v_bd))
            den_loc.append(_edot_r(scores, seg_ones_l, n=2))
            kw_c = kw[rws[c], :]
            kw_t = kw_c.T
            c_loc.append([jnp.where(pair_mask, _bdot(kw_t[pws[p], :], v_cc[:, pws[p]]), 0.0)
                          for p in range(N_PAIR_C)])
            n_loc.append(jnp.sum(kw_c, axis=0, keepdims=True))
            yield

        m_in = []
        m_cur = mrow_s[...]
        for c in CH:
            m_in.append(m_cur)
            m_cur = jnp.maximum(b_end[c] + m_cur, m_loc_end[c])
        m_out = m_in[1:] + [m_cur]
        mrow_s[...] = m_cur
        scale_rows = jnp.concatenate(
            [jnp.exp(b_end[c] + m_in[c] - m_out[c]) for c in CH]
            + [jnp.exp(m_loc_end[c] - m_out[c]) for c in CH], axis=0)
        scale_exp = expand(scale_rows)
        yield
        c_in, n_in = [], []
        c_cur = [cbd_s[p] for p in range(N_PAIR_C)]
        n_cur = nrow_s[...]
        for c in CH:
            c_in.append(c_cur)
            n_in.append(n_cur)
            w0 = scale_exp[c:c + 1, :]
            wl = scale_exp[NC + c:NC + c + 1, :]
            c_cur = [c_cur[p] * w0[:, pws[p]] + c_loc[c][p] * wl[:, pws[p]] for p in range(N_PAIR_C)]
            n_cur = n_cur * w0 + n_loc[c] * wl
        for p in range(N_PAIR_C):
            cbd_s[p] = c_cur[p]
        nrow_s[...] = n_cur
        for h in range(N_HEAD_C):
            mc_o[0, h] = c_cur[h // 2][sls[h % 2], sls[h % 2]]
            mn_o[0, h:h + 1, :] = n_cur[:, sls[h]]
        mm_o[0] = m_cur[:, 0:N_HEAD_C]

        yield
        m0_b = jnp.concatenate([jnp.broadcast_to(m_in[c], (L, GATE_W)) for c in CH], axis=0)
        log_inter = b_col + m0_b
        m_t = jnp.maximum(log_inter, m_loc)
        w_inter = jnp.exp(log_inter - m_t)
        w_local = jnp.exp(m_loc - m_t)
        q_all = ml_s[0]
        qc0 = jnp.concatenate(
            [jnp.concatenate([_bdot(q_all[rws[c], pws[p]], c_in[c][p]) for p in range(N_PAIR_C)], axis=1)
             for c in CH], axis=0)
        qn0 = jnp.concatenate([_edot_r(q_all[rws[c], :] * n_in[c], seg_ones) for c in CH], axis=0)
        yield
        den = w_inter * qn0 + w_local * jnp.concatenate(den_loc, axis=0)
        inv_den = 1.0 / jnp.maximum(jnp.abs(den), jnp.exp(-m_t))
        h_c = (expand(w_inter * inv_den) * qc0
               + expand(w_local * inv_den) * jnp.concatenate(p_loc, axis=0))

        h_cen = h_c - segsum(h_c) * (1.0 / HEAD_DIM)
        yield
        h_var = segsum(h_cen * h_cen) * (1.0 / HEAD_DIM)
        zc = proj_s[HIST:HIST + TT, OFF_PC + 2 * D_C:OFF_PC + 3 * D_C]
        yc_out = _sigmoid(zc) * (h_cen * lax.rsqrt(h_var + GN_EPS_C) * vec(V_GN, D_C))
        yield
        out_parts['c'] = jnp.dot(yc_out.astype(BF16), w_out_ref[0, D_A + D_B:D_MODEL, :],
                                 preferred_element_type=F32)

    pending = [group_a(), group_b(), group_c()]
    while pending:
        pending = [g for g in pending if next(g, StopIteration) is not StopIteration]

    last3 = proj_s[HIST + TT - 3:HIST + TT, :]
    proj_s[HIST - 3:HIST, :] = last3
    ca_o[0] = last3[:, 0:D_A]
    sh_o[0] = last3[2:3, OFF_PB:OFF_PB + D_B_IN]
    cc_o[0] = last3[:, OFF_PC:OFF_PC + D_C]

    xo_ref[0] = x_ref[0] + (out_parts['a'] + out_parts['b']) + out_parts['c']


def _mixer_call(layer, x, states, wts, TT, L):
    B, T, _ = x.shape
    NT = T // TT

    def state_spec(shape):
        nd = len(shape)
        return pl.BlockSpec((1,) + tuple(shape[1:]), lambda b, j: (b,) + (0,) * (nd - 1))

    def layer_spec(arr):
        nd = arr.ndim
        return pl.BlockSpec((1,) + tuple(arr.shape[1:]), lambda b, j: (layer,) + (0,) * (nd - 1))

    x_spec = pl.BlockSpec((1, TT, D_MODEL), lambda b, j: (b, j, 0))
    state_specs = [state_spec(s.shape) for s in states]
    w_names = ('vec', 'w_in', 'lruw', 'w2', 'a2', 'g2', 'wqk', 'wif', 'w_out')
    w_arrs = [wts[n] for n in w_names]

    out_shape = ([jax.ShapeDtypeStruct(x.shape, F32)]
                 + [jax.ShapeDtypeStruct(s.shape, F32) for s in states])
    scratch = [
        pltpu.VMEM((HIST + TT, D_IN), F32),
        pltpu.VMEM((7, TT, D_B), F32),
        pltpu.VMEM((2, TT, D_C), F32),
        pltpu.VMEM((TT, D_B), F32),
        pltpu.VMEM((N_PAIR_C, PAIR_W, PAIR_W), F32),
        pltpu.VMEM((1, D_C), F32),
        pltpu.VMEM((1, GATE_W), F32),
    ]
    outs = pl.pallas_call(
        functools.partial(_mixer_kernel, TT, L),
        grid=(B, NT),
        in_specs=[x_spec] + state_specs + [layer_spec(a) for a in w_arrs],
        out_specs=[x_spec] + state_specs,
        out_shape=out_shape,
        scratch_shapes=scratch,
        compiler_params=pltpu.CompilerParams(
            dimension_semantics=("arbitrary", "arbitrary"),
            vmem_limit_bytes=VMEM_LIMIT_MIXER),
        name="mixer",
    )(x, *states, *w_arrs)
    return outs[0], tuple(outs[1:])


FF_CHUNK = 768


def _ffn_kernel(final, x_ref, vec_ref, wi_ref, wo_ref, o_ref):
    x = x_ref[...]
    hn = _rms(x, vec_ref[0, 0:1, :]).astype(BF16)
    acc = x
    for lo in range(0, D_FF, FF_CHUNK):
        hi = min(lo + FF_CHUNK, D_FF)
        gate = jnp.dot(hn, wi_ref[0, :, lo:hi], preferred_element_type=F32)
        up = jnp.dot(hn, wi_ref[0, :, D_FF + lo:D_FF + hi], preferred_element_type=F32)
        act = (gate * _sigmoid(gate) * up).astype(BF16)
        acc = acc + jnp.dot(act, wo_ref[0, lo:hi, :], preferred_element_type=F32)
    if final:
        acc = _rms(acc, vec_ref[0, 1:2, :])
    o_ref[...] = acc


def _ffn_call(layer, x2d, wts, TM, final):
    M = x2d.shape[0]
    return pl.pallas_call(
        functools.partial(_ffn_kernel, final),
        grid=(M // TM,),
        in_specs=[
            pl.BlockSpec((TM, D_MODEL), lambda i: (i, 0)),
            pl.BlockSpec((1, 8, D_MODEL), lambda i: (layer, 0, 0)),
            pl.BlockSpec((1, D_MODEL, 2 * D_FF), lambda i: (layer, 0, 0),
                         pipeline_mode=pl.Buffered(1)),
            pl.BlockSpec((1, D_FF, D_MODEL), lambda i: (layer, 0, 0),
                         pipeline_mode=pl.Buffered(1)),
        ],
        out_specs=pl.BlockSpec((TM, D_MODEL), lambda i: (i, 0)),
        out_shape=jax.ShapeDtypeStruct(x2d.shape, F32),
        compiler_params=pltpu.CompilerParams(
            dimension_semantics=("arbitrary",),
            vmem_limit_bytes=VMEM_LIMIT_FFN),
        name="ffn",
    )(x2d, wts['ffn_vec'], wts['w_ffn_in'], wts['w_ffn_out'])


def _block_diag(w):
    P, n, d, e = w.shape
    out = jnp.zeros((P, n * d, n * e), w.dtype)
    for i in range(n):
        out = out.at[:, i * d:(i + 1) * d, i * e:(i + 1) * e].set(w[:, i])
    return out


def _pack_rows(P, rows):
    table = jnp.zeros((P, N_VEC, VEC_W), F32)
    for r, arr in rows:
        arr = arr.reshape(P, -1, arr.shape[-1]).astype(F32)
        table = table.at[:, r:r + arr.shape[1], 0:arr.shape[2]].set(arr)
    return table


def _run_stream(x, states, wts, TT, L, TM):
    B, T, _ = x.shape
    new_states = []
    for l in range(DEPTH):
        st_l = tuple(s[l] for s in states)
        x, st_new = _mixer_call(l, x, st_l, wts, TT, L)
        new_states.append(st_new)
        x = _ffn_call(l, x.reshape(B * T, D_MODEL), wts, TM, l == DEPTH - 1).reshape(B, T, D_MODEL)
    stacked = tuple(jnp.stack([st[i] for st in new_states]) for i in range(len(states)))
    return x, stacked


def kernel(x_prompt, x_sample, state_conv_a, state_lru, state_shift_b, state_wkv, state_conv_c, state_mem_c, state_mem_n, state_mem_m, norm1, w_in, conv_a_w, conv_a_b, lru_wr, lru_br, lru_wi, lru_bi, lru_lambda, norm_a, rwkv_mu, rwkv_w0, rwkv_w2, rwkv_a0, rwkv_a2, rwkv_g2, rwkv_kk, rwkv_ka, rwkv_rk, rwkv_lnw, rwkv_lnb, conv_c_w, conv_c_b, mlstm_wq, mlstm_wk, mlstm_wif, mlstm_bif, mlstm_gn, w_out, norm2, w_ffn_in, w_ffn_out, norm_f):
    P = DEPTH
    wif_pad = (jnp.zeros((P, 3 * D_C, 2 * GATE_W), F32)
               .at[:, :, 0:N_HEAD_C].set(mlstm_wif[:, :, 0:N_HEAD_C])
               .at[:, :, GATE_W:GATE_W + N_HEAD_C].set(mlstm_wif[:, :, N_HEAD_C:]))
    wts = {
        'vec': _pack_rows(P, [
            (V_NORM1, norm1), (V_CONV_A_B, conv_a_b),
            (V_LRU_B, jnp.concatenate([lru_br, lru_bi], axis=-1)), (V_LAM, lru_lambda),
            (V_NORM_A, norm_a), (V_MU, rwkv_mu), (V_W0, rwkv_w0), (V_A0, rwkv_a0),
            (V_KK, rwkv_kk), (V_KA, rwkv_ka), (V_RK, rwkv_rk.reshape(P, D_B)),
            (V_LNW, rwkv_lnw), (V_LNB, rwkv_lnb), (V_CONV_C_B, conv_c_b),
            (V_BIF, mlstm_bif[:, 0:N_HEAD_C]), (V_BIF2, mlstm_bif[:, N_HEAD_C:]), (V_GN, mlstm_gn),
            (V_CONV_A_W, conv_a_w), (V_CONV_C_W, conv_c_w)]),
        'w_in': w_in.astype(BF16),
        'lruw': jnp.concatenate([_block_diag(lru_wr), _block_diag(lru_wi)], axis=-1).astype(BF16),
        'w2': rwkv_w2.astype(BF16),
        'a2': rwkv_a2.astype(BF16),
        'g2': rwkv_g2.astype(BF16),
        'wqk': jnp.concatenate([_block_diag(mlstm_wq), _block_diag(mlstm_wk)], axis=-1).astype(BF16),
        'wif': wif_pad.astype(BF16),
        'w_out': w_out.astype(BF16),
        'ffn_vec': jnp.zeros((P, 8, D_MODEL), F32).at[:, 0].set(norm2).at[:, 1].set(
            jnp.broadcast_to(norm_f, (P, D_MODEL))),
        'w_ffn_in': w_ffn_in.astype(BF16),
        'w_ffn_out': w_ffn_out.astype(BF16),
    }

    def to_kernel_layout(conv_a, lru, shift, wkv, conv_c, mem_c, mem_n, mem_m):
        Bn = lru.shape[1]
        return (conv_a, lru.reshape(P, Bn, 1, D_A), shift, jnp.swapaxes(wkv, -1, -2), conv_c,
                mem_c, mem_n, mem_m.reshape(P, Bn, 1, N_HEAD_C))

    def from_kernel_layout(conv_a, lru, shift, wkv, conv_c, mem_c, mem_n, mem_m):
        Bn = lru.shape[1]
        return (conv_a, lru.reshape(P, Bn, D_A), shift, jnp.swapaxes(wkv, -1, -2), conv_c,
                mem_c, mem_n, mem_m.reshape(P, Bn, N_HEAD_C))

    Bp, Tp, _ = x_prompt.shape
    Bs, Ts, _ = x_sample.shape
    zero_states = (jnp.zeros((P, Bp, CONV_W - 1, D_A), F32),
                   jnp.zeros((P, Bp, D_A), F32),
                   jnp.zeros((P, Bp, 1, D_B_IN), F32),
                   jnp.zeros((P, Bp, N_HEAD_B, HEAD_DIM, HEAD_DIM), F32),
                   jnp.zeros((P, Bp, CONV_W - 1, D_C), F32),
                   jnp.zeros((P, Bp, N_HEAD_C, HEAD_DIM, HEAD_DIM), F32),
                   jnp.zeros((P, Bp, N_HEAD_C, HEAD_DIM), F32),
                   jnp.zeros((P, Bp, N_HEAD_C), F32))
    Lp = MLSTM_CHUNK if Tp % MLSTM_CHUNK == 0 else Tp
    Ls = MLSTM_CHUNK if Ts % MLSTM_CHUNK == 0 else Ts
    TTp = 256 if Tp % 256 == 0 else Lp
    y_prompt, p_states = _run_stream(x_prompt, to_kernel_layout(*zero_states), wts, TTp, Lp, 512)
    y_sample, s_states = _run_stream(
        x_sample, to_kernel_layout(state_conv_a, state_lru, state_shift_b, state_wkv,
                                   state_conv_c, state_mem_c, state_mem_n, state_mem_m),
        wts, Ls, Ls, Bs * Ts)
    return (y_prompt, y_sample) + from_kernel_layout(*p_states) + from_kernel_layout(*s_states)
```

```python
import functools
import math

import jax
import jax.numpy as jnp
from jax import lax
from jax.experimental import pallas as pl
from jax.experimental.pallas import tpu as pltpu

F32 = jnp.float32
BF16 = jnp.bfloat16

D_MODEL = 1024
DEPTH = 4
HEAD_DIM = 64
D_A = 256
N_BLK_A = 4
CONV_W = 4
LRU_C = 8.0
D_B = 384
N_HEAD_B = 6
LORA_W = 64
LORA_A = 64
LORA_G = 128
D_B_IN = 3 * D_B + LORA_W + LORA_A + LORA_G
D_C = 384
N_HEAD_C = 6
D_IN = 2 * D_A + D_B_IN + 3 * D_C
D_FF = 2816
RMS_EPS = 1e-6
GN_EPS_B = 64e-5
GN_EPS_C = 1e-6
MLSTM_CHUNK = 64

OFF_PB = 2 * D_A
OFF_PC = OFF_PB + D_B_IN
IN_SPLITS = ((0, OFF_PB), (OFF_PB, 2048), (2048, D_IN))
HIST = 8
GATE_W = 128
PAIR_W = 2 * HEAD_DIM
N_PAIR_C = N_HEAD_C // 2

(V_NORM1, V_CONV_A_B, V_LRU_B, V_LAM, V_NORM_A, V_MU, V_W0, V_A0, V_KK, V_KA, V_RK,
 V_LNW, V_LNB, V_CONV_C_B, V_BIF, V_GN, V_CONV_A_W, V_CONV_C_W, V_BIF2) = (
     0, 1, 2, 3, 4, 5, 6, 7, 8, 9, 10, 11, 12, 13, 14, 15, 16, 20, 24)
N_VEC = 32
VEC_W = D_B_IN

VMEM_LIMIT_MIXER = 48 * 1024 * 1024
VMEM_LIMIT_FFN = 56 * 1024 * 1024


def _bdot(a, b):
    return jnp.dot(a.astype(BF16), b.astype(BF16), preferred_element_type=F32)


def _bdot_nt(a, b):
    return lax.dot_general(a.astype(BF16), b.astype(BF16), (((1,), (1,)), ((), ())),
                           preferred_element_type=F32)


def _split(a, n):
    parts = []
    rem = a
    for i in range(n):
        p = rem.astype(BF16)
        parts.append(p)
        if i + 1 < n:
            rem = rem - p.astype(F32)
    return parts


def _edot_r(a, e, n=3):
    acc = None
    for p in _split(a, n):
        t = jnp.dot(p, e, preferred_element_type=F32)
        acc = t if acc is None else acc + t
    return acc


def _edot_l(e, a, n=3):
    acc = None
    for p in _split(a, n):
        t = jnp.dot(e, p, preferred_element_type=F32)
        acc = t if acc is None else acc + t
    return acc


def _xdot(a, b, nt=False):
    return _bdot_nt(a, b) if nt else _bdot(a, b)


def _softplus(z):
    return jnp.maximum(z, 0.0) + jnp.log(1.0 + jnp.exp(-jnp.abs(z)))


def _sigmoid(z):
    return 0.5 * jnp.tanh(0.5 * z) + 0.5


def _rms(x, g):
    return x * lax.rsqrt(jnp.mean(x * x, axis=-1, keepdims=True) + RMS_EPS) * g


def _iota2(shape, dim):
    return lax.broadcasted_iota(jnp.int32, shape, dim)


def _mixer_kernel(TT, L,
                  x_ref, ca_ref, lru_ref, sh_ref, wkv_ref, cc_ref, mc_ref, mn_ref, mm_ref,
                  vec_ref, w_in_ref, lruw_ref, w2_ref, a2_ref, g2_ref, wqk_ref, wif_ref, w_out_ref,
                  xo_ref, ca_o, lru_o, sh_o, wkv_o, cc_o, mc_o, mn_o, mm_o,
                  proj_s, rk_s, ml_s, yb_s, ybc_s, wkv_s, cbd_s, nrow_s, mrow_s):
    j = pl.program_id(1)
    NC = TT // L
    LOG_L = int(math.log2(L))
    CH = range(NC)
    HB = range(N_HEAD_B)
    sls = [slice(h * HEAD_DIM, (h + 1) * HEAD_DIM) for h in HB]
    rws = [slice(c * L, (c + 1) * L) for c in CH]

    def vec(row, width):
        return vec_ref[0, row:row + 1, 0:width]

    @pl.when(j == 0)
    def _init():
        proj_s[0:HIST, :] = jnp.zeros((HIST, D_IN), F32)
        proj_s[HIST - 3:HIST, 0:D_A] = ca_ref[0]
        proj_s[HIST - 1:HIST, OFF_PB:OFF_PB + D_B_IN] = sh_ref[0]
        proj_s[HIST - 3:HIST, OFF_PC:OFF_PC + D_C] = cc_ref[0]
        lru_o[...] = lru_ref[...]
        for h in HB:
            wkv_s[h] = wkv_ref[0, h].T
        cbd_s[...] = jnp.zeros((N_PAIR_C, PAIR_W, PAIR_W), F32)
        mrow_s[...] = jnp.zeros((1, GATE_W), F32)
        for h in range(N_HEAD_C):
            cbd_s[h // 2, sls[h % 2], sls[h % 2]] = mc_ref[0, h]
            nrow_s[:, sls[h]] = mn_ref[0, h:h + 1, :]
        mrow_s[:, 0:N_HEAD_C] = mm_ref[0]

    x = x_ref[0]
    hn = _rms(x, vec(V_NORM1, D_MODEL))
    hn_b = hn.astype(BF16)
    for lo, hi in IN_SPLITS:
        proj_s[HIST:HIST + TT, lo:hi] = jnp.dot(hn_b, w_in_ref[0, :, lo:hi],
                                                preferred_element_type=F32)
    out_parts = {}

    def taps(col0, width, w_row0):
        acc = None
        for t in range(CONV_W):
            r0 = HIST - (CONV_W - 1) + t
            term = vec(w_row0 + t, width) * proj_s[r0:r0 + TT, col0:col0 + width]
            acc = term if acc is None else acc + term
        return acc

    row_t = _iota2((TT, 1), 0)

    seg_r = _iota2((D_B, D_B), 0) >> 6
    seg_c = _iota2((D_B, D_B), 1) >> 6
    ones_bd = jnp.where(seg_r == seg_c, 1.0, 0.0).astype(BF16)
    seg_ones = jnp.where((_iota2((D_C, GATE_W), 0) >> 6) == _iota2((D_C, GATE_W), 1),
                         1.0, 0.0).astype(BF16)
    expand_m = jnp.where(_iota2((GATE_W, D_C), 0) == (_iota2((GATE_W, D_C), 1) >> 6),
                         1.0, 0.0).astype(BF16)
    tt_r = _iota2((TT, TT), 0)
    tt_c = _iota2((TT, TT), 1)
    same_chunk = (tt_r >> LOG_L) == (tt_c >> LOG_L)
    blk_tril = jnp.where(same_chunk & (tt_r >= tt_c), 1.0, 0.0).astype(BF16)

    def segsum(z):
        return _edot_r(z, ones_bd, n=1)

    def expand(z, n=2):
        return _edot_r(z, expand_m, n=n)

    def group_a():
        xa = taps(0, D_A, V_CONV_A_W) + vec(V_CONV_A_B, D_A)
        gates_a = _bdot(xa, lruw_ref[0]) + vec(V_LRU_B, 2 * D_A)
        gate_r = _sigmoid(gates_a[:, 0:D_A])
        gate_i = _sigmoid(gates_a[:, D_A:2 * D_A])
        log_a = (-LRU_C) * gate_r * _softplus(-vec(V_LAM, D_A))
        a_cum = jnp.exp(log_a)
        one_m_a2 = 1.0 - jnp.exp(2.0 * log_a)
        h_loc = jnp.where(one_m_a2 > 0.0, one_m_a2 * lax.rsqrt(one_m_a2), 0.0) * (gate_i * xa)
        yield
        d = 1
        while d < TT:
            keep = row_t >= d
            a_sh = jnp.where(keep, pltpu.roll(a_cum, d, 0), 1.0)
            h_sh = jnp.where(keep, pltpu.roll(h_loc, d, 0), 0.0)
            h_loc = a_cum * h_sh + h_loc
            a_cum = a_cum * a_sh
            d *= 2
            yield
        h_lru = h_loc + a_cum * lru_o[0]
        lru_o[0] = h_lru[TT - 1:TT, :]
        pag = proj_s[HIST:HIST + TT, D_A:2 * D_A]
        gelu = 0.5 * pag * (1.0 + jnp.tanh(math.sqrt(2.0 / math.pi)
                                           * (pag + 0.044715 * (pag * pag * pag))))
        ya = _rms(h_lru, vec(V_NORM_A, D_A)) * gelu
        yield
        out_parts['a'] = jnp.dot(ya.astype(BF16), w_out_ref[0, 0:D_A, :],
                                 preferred_element_type=F32)

    def group_b():
        pb = proj_s[HIST:HIST + TT, OFF_PB:OFF_PB + D_B_IN]
        prev = proj_s[HIST - 1:HIST - 1 + TT, OFF_PB:OFF_PB + D_B_IN]
        xs = pb + (prev - pb) * vec(V_MU, D_B_IN)
        r_b = xs[:, 0:D_B]
        k_b = xs[:, D_B:2 * D_B]
        v_b = xs[:, 2 * D_B:3 * D_B]
        o3 = 3 * D_B
        wd = xs[:, o3:o3 + LORA_W]
        ad = xs[:, o3 + LORA_W:o3 + LORA_W + LORA_A]
        gd = xs[:, o3 + LORA_W + LORA_A:D_B_IN]
        w_log = -_softplus(-(vec(V_W0, D_B) + _bdot(jnp.tanh(wd), w2_ref[0]))) - 0.5
        logw = -jnp.exp(w_log)
        a_b = _sigmoid(vec(V_A0, D_B) + _bdot(ad, a2_ref[0]))
        g_b = _bdot(_sigmoid(gd), g2_ref[0])
        yield
        kk = k_b * vec(V_KK, D_B)
        kk = kk * lax.rsqrt(jnp.maximum(segsum(kk * kk), 1e-24))
        k_b = k_b * (1.0 + (a_b - 1.0) * vec(V_KA, D_B))
        bonus = segsum(r_b * k_b * vec(V_RK, D_B)) * v_b
        ka_b = kk * a_b
        yield
        cl = _edot_l(blk_tril, logw)
        cl_last = [cl[c * L + L - 1:c * L + L, :] for c in CH]
        ctot = jnp.concatenate([jnp.broadcast_to(cl_last[c], (L, D_B)) for c in CH], axis=0)
        w_inv = jnp.exp(-cl)
        to_end = jnp.exp(ctot - cl)
        rk_s[0] = kk * jnp.exp(cl - logw)
        rk_s[1] = k_b * w_inv
        rk_s[2] = ka_b * w_inv
        rk_s[3] = r_b * jnp.exp(cl)
        rk_s[4] = v_b
        rk_s[5] = k_b * to_end
        rk_s[6] = ka_b * to_end
        w_end = [jnp.exp(cl_last[c]) for c in CH]

        yield
        e64 = _iota2((HEAD_DIM, HEAD_DIM), 0) == _iota2((HEAD_DIM, HEAD_DIM), 1)
        GROUPS = ((0, 4), (4, 2))

        def group_consts(nh):
            nl, w = nh * L, nh * HEAD_DIM
            t_l = _iota2((L, nl), 0)
            s_l = _iota2((L, nl), 1) & (L - 1)
            return dict(
                nat=jnp.where((_iota2((nl, w), 0) >> LOG_L) == (_iota2((nl, w), 1) >> 6),
                              1.0, 0.0).astype(BF16),
                sq=jnp.where((_iota2((nl, nl), 0) >> LOG_L) == (_iota2((nl, nl), 1) >> LOG_L),
                             1.0, 0.0).astype(BF16),
                t=t_l, s=s_l, strict=t_l > s_l, incl=t_l >= s_l,
                eye=jnp.where(t_l == s_l, 1.0, 0.0))

        gconst = {nh: group_consts(nh) for _, nh in GROUPS}

        def bdiag(z, mask):
            zb = z.astype(BF16)
            return jnp.concatenate([zb] * (mask.shape[0] // L), axis=0) * mask

        def mm(a_, b_bf16):
            return jnp.dot(a_.astype(BF16), b_bf16, preferred_element_type=F32)

        def mm_nt(a_, b_bf16):
            return lax.dot_general(a_.astype(BF16), b_bf16, (((1,), (1,)), ((), ())),
                                   preferred_element_type=F32)

        CG = [(c, g) for c in CH for g in range(len(GROUPS))]

        def gslice(i, c, g):
            h0, nh = GROUPS[g]
            return rk_s[i, rws[c], h0 * HEAD_DIM:(h0 + nh) * HEAD_DIM]

        def gc(g):
            return gconst[GROUPS[g][1]]

        qg = {p: gslice(0, *p) for p in CG}
        rg = {p: gslice(3, *p) for p in CG}
        vg = {p: gslice(4, *p) for p in CG}
        qr = {p: jnp.concatenate([qg[p], rg[p]], axis=0) for p in CG}
        sa = {(c, g): mm_nt(qr[c, g], bdiag(gslice(2, c, g), gc(g)['nat'])) for c, g in CG}
        sk = {(c, g): mm_nt(qr[c, g], bdiag(gslice(1, c, g), gc(g)['nat'])) for c, g in CG}
        n_m = {(c, g): jnp.where(gc(g)['strict'], sa[c, g][0:L], 0.0) for c, g in CG}
        mra_m = {(c, g): jnp.where(gc(g)['incl'], sa[c, g][L:2 * L], 0.0) for c, g in CG}
        mk2 = {(c, g): jnp.concatenate([jnp.where(gc(g)['strict'], sk[c, g][0:L], 0.0),
                                        jnp.where(gc(g)['incl'], sk[c, g][L:2 * L], 0.0)], axis=0)
               for c, g in CG}
        yield
        mv2 = {(c, g): mm(mk2[c, g], bdiag(vg[c, g], gc(g)['nat'])) for c, g in CG}
        mkv = {p: mv2[p][0:L] for p in CG}
        mrkv = {p: mv2[p][L:2 * L] for p in CG}
        yield
        xinv = {(c, g): gc(g)['eye'] - jnp.where(((gc(g)['t'] & 1) == 1) & (gc(g)['s'] == gc(g)['t'] - 1),
                                                  n_m[c, g], 0.0) for c, g in CG}
        for lg in range(1, LOG_L):
            lvl = {nh: (((k['t'] >> lg) & 1) == 1) & ((k['s'] >> lg) == (k['t'] >> lg) - 1)
                   for nh, k in gconst.items()}
            xc = {(c, g): mm(xinv[c, g], bdiag(jnp.where(lvl[GROUPS[g][1]], n_m[c, g], 0.0), gc(g)['sq']))
                  for c, g in CG}
            yield
            xinv = {(c, g): xinv[c, g] - mm(xc[c, g], bdiag(xinv[c, g], gc(g)['sq'])) for c, g in CG}
            yield
        p1 = {(c, g): mm(xinv[c, g], bdiag(qg[c, g], gc(g)['nat'])) for c, g in CG}
        p2 = {(c, g): mm(xinv[c, g], bdiag(mkv[c, g], gc(g)['nat'])) for c, g in CG}
        yield
        rq = {(c, g): rg[c, g] - mm(mra_m[c, g], bdiag(p1[c, g], gc(g)['nat'])) for c, g in CG}
        yc = {(c, g): mrkv[c, g] - mm(mra_m[c, g], bdiag(p2[c, g], gc(g)['nat'])) for c, g in CG}

        yield
        CHH = [(c, h) for c in CH for h in HB]

        def head_of(d, c, h):
            g = 0 if h < GROUPS[1][0] else 1
            i = h - GROUPS[g][0]
            return d[c, g][:, i * HEAD_DIM:(i + 1) * HEAD_DIM]

        khat_t = [rk_s[5, rws[c], :].T for c in CH]
        ahat_t = [rk_s[6, rws[c], :].T for c in CH]
        akp = {(c, h): _xdot(jnp.concatenate([ahat_t[c][sls[h], :], khat_t[c][sls[h], :]], axis=0),
                             jnp.concatenate([head_of(p1, c, h), head_of(p2, c, h), head_of(vg, c, h)],
                                             axis=1)) for c, h in CHH}
        g_m = {p: akp[p][0:HEAD_DIM, 0:HEAD_DIM] for p in CHH}
        h_m = {p: akp[p][HEAD_DIM:2 * HEAD_DIM, 2 * HEAD_DIM:3 * HEAD_DIM]
               - akp[p][0:HEAD_DIM, HEAD_DIM:2 * HEAD_DIM] for p in CHH}
        w_end_col = {(c, h): jnp.sum(jnp.where(e64, w_end[c][:, sls[h]], 0.0), axis=1, keepdims=True)
                     for c, h in CHH}
        yield
        s_t = [wkv_s[h] for h in HB]
        for c in CH:
            y_h = [_xdot(head_of(rq, c, h), s_t[h]) + head_of(yc, c, h) for h in HB]
            s_t = [w_end_col[c, h] * s_t[h] - _xdot(g_m[c, h], s_t[h]) + h_m[c, h] for h in HB]
            for h in HB:
                yb_s[rws[c], sls[h]] = y_h[h]
            yield
        for h in HB:
            wkv_s[h] = s_t[h]
            wkv_o[0, h] = s_t[h].T

        y_b = yb_s[...]
        y_cen = y_b - segsum(y_b) * (1.0 / HEAD_DIM)
        yield
        y_var = segsum(y_cen * y_cen) * (1.0 / HEAD_DIM)
        y_n = y_cen * lax.rsqrt(y_var + GN_EPS_B) * vec(V_LNW, D_B) + vec(V_LNB, D_B)
        yb_out = (y_n + bonus) * g_b
        yield
        ybc_s[:, 0:D_B] = yb_out

    def group_c():
        xconv = taps(OFF_PC, D_C, V_CONV_C_W) + vec(V_CONV_C_B, D_C)
        xact = xconv * _sigmoid(xconv)
        qk = _bdot(xact, wqk_ref[0])
        q_c = qk[:, 0:D_C]
        k_c = qk[:, D_C:2 * D_C]
        vc = proj_s[HIST:HIST + TT, OFF_PC + D_C:OFF_PC + 2 * D_C]
        gates_c = (_bdot(q_c, wif_ref[0, 0:D_C, :]) + _bdot(k_c, wif_ref[0, D_C:2 * D_C, :])
                   + _bdot(vc, wif_ref[0, 2 * D_C:3 * D_C, :]))
        i_pre = gates_c[:, 0:GATE_W] + vec(V_BIF, GATE_W)
        logf = -_softplus(-(gates_c[:, GATE_W:2 * GATE_W] + vec(V_BIF2, GATE_W)))
        ml_s[0] = q_c
        ml_s[1] = k_c * (HEAD_DIM ** -0.5)

        yield
        b_col = _edot_l(blk_tril, logf)
        yield
        g_col = i_pre - b_col
        g_max = g_col
        pos = row_t & (L - 1)
        d = 1
        while d < L:
            g_max = jnp.where(pos >= d, jnp.maximum(g_max, pltpu.roll(g_max, d, 0)), g_max)
            d *= 2
        yield
        m_loc = b_col + g_max
        g_last = [g_max[c * L + L - 1:c * L + L, :] for c in CH]
        b_end = [b_col[c * L + L - 1:c * L + L, :] for c in CH]
        m_loc_end = [b_end[c] + g_last[c] for c in CH]
        w_end_loc = jnp.concatenate([jnp.exp(g_col[rws[c], :] - g_last[c]) for c in CH], axis=0)
        kw = ml_s[1] * expand(w_end_loc, n=1)
        yield
        s_of_lane = _iota2((L, N_HEAD_C * L), 1) & (L - 1)
        t_of_row = _iota2((L, N_HEAD_C * L), 0)
        causal_t = t_of_row >= s_of_lane
        eye_t = t_of_row == s_of_lane
        bd_l = (_iota2((N_HEAD_C * L, D_C), 0) >> LOG_L) == (_iota2((N_HEAD_C * L, D_C), 1) >> 6)
        seg_ones_l = jnp.where((_iota2((N_HEAD_C * L, GATE_W), 0) >> LOG_L)
                               == _iota2((N_HEAD_C * L, GATE_W), 1), 1.0, 0.0).astype(BF16)
        expand_l = jnp.where(_iota2((GATE_W, N_HEAD_C * L), 0)
                             == (_iota2((GATE_W, N_HEAD_C * L), 1) >> LOG_L), 1.0, 0.0).astype(BF16)
        g_exp_l = _edot_r(g_col, expand_l, n=2)
        g_max_exp_l = _edot_r(g_max, expand_l, n=2)

        pws = [slice(p * PAIR_W, (p + 1) * PAIR_W) for p in range(N_PAIR_C)]
        pair_mask = (_iota2((PAIR_W, PAIR_W), 0) >> 6) == (_iota2((PAIR_W, PAIR_W), 1) >> 6)
        yield
        p_loc, den_loc, c_loc, n_loc = [], [], [], []
        for c in CH:
            q_cc = ml_s[0, rws[c], :]
            k_cc = ml_s[1, rws[c], :]
            v_cc = proj_s[HIST + c * L:HIST + (c + 1) * L, OFF_PC + D_C:OFF_PC + 2 * D_C]
            k_bd = jnp.where(bd_l, jnp.concatenate([k_cc] * N_HEAD_C, axis=0), 0.0)
            v_bd = jnp.where(bd_l, jnp.concatenate([v_cc] * N_HEAD_C, axis=0), 0.0)
            g_flat = jnp.sum(jnp.where(eye_t, g_exp_l[rws[c], :], 0.0), axis=0, keepdims=True)
            dmat = jnp.where(causal_t, jnp.exp(jnp.minimum(g_flat - g_max_exp_l[rws[c], :], 0.0)), 0.0)
            scores = _bdot_nt(q_cc, k_bd) * dmat
            p_loc.append(_bdot(scores, v_bd))
            den_loc.append(_edot_r(scores, seg_ones_l, n=1))
            kw_c = kw[rws[c], :]
            kw_t = kw_c.T
            c_loc.append([jnp.where(pair_mask, _bdot(kw_t[pws[p], :], v_cc[:, pws[p]]), 0.0)
                          for p in range(N_PAIR_C)])
            n_loc.append(jnp.sum(kw_c, axis=0, keepdims=True))
            yield

        m_in = []
        m_cur = mrow_s[...]
        for c in CH:
            m_in.append(m_cur)
            m_cur = jnp.maximum(b_end[c] + m_cur, m_loc_end[c])
        m_out = m_in[1:] + [m_cur]
        mrow_s[...] = m_cur
        scale_rows = jnp.concatenate(
            [jnp.exp(b_end[c] + m_in[c] - m_out[c]) for c in CH]
            + [jnp.exp(m_loc_end[c] - m_out[c]) for c in CH], axis=0)
        scale_exp = expand(scale_rows)
        yield
        c_in, n_in = [], []
        c_cur = [cbd_s[p] for p in range(N_PAIR_C)]
        n_cur = nrow_s[...]
        for c in CH:
            c_in.append(c_cur)
            n_in.append(n_cur)
            w0 = scale_exp[c:c + 1, :]
            wl = scale_exp[NC + c:NC + c + 1, :]
            c_cur = [c_cur[p] * w0[:, pws[p]] + c_loc[c][p] * wl[:, pws[p]] for p in range(N_PAIR_C)]
            n_cur = n_cur * w0 + n_loc[c] * wl
        for p in range(N_PAIR_C):
            cbd_s[p] = c_cur[p]
        nrow_s[...] = n_cur
        for h in range(N_HEAD_C):
            mc_o[0, h] = c_cur[h // 2][sls[h % 2], sls[h % 2]]
            mn_o[0, h:h + 1, :] = n_cur[:, sls[h]]
        mm_o[0] = m_cur[:, 0:N_HEAD_C]

        yield
        m0_b = jnp.concatenate([jnp.broadcast_to(m_in[c], (L, GATE_W)) for c in CH], axis=0)
        log_inter = b_col + m0_b
        m_t = jnp.maximum(log_inter, m_loc)
        w_inter = jnp.exp(log_inter - m_t)
        w_local = jnp.exp(m_loc - m_t)
        q_all = ml_s[0]
        qc0 = jnp.concatenate(
            [jnp.concatenate([_bdot(q_all[rws[c], pws[p]], c_in[c][p]) for p in range(N_PAIR_C)], axis=1)
             for c in CH], axis=0)
        qn0 = jnp.concatenate([_edot_r(q_all[rws[c], :] * n_in[c], seg_ones, n=1) for c in CH], axis=0)
        yield
        den = w_inter * qn0 + w_local * jnp.concatenate(den_loc, axis=0)
        inv_den = 1.0 / jnp.maximum(jnp.abs(den), jnp.exp(-m_t))
        h_c = (expand(w_inter * inv_den) * qc0
               + expand(w_local * inv_den) * jnp.concatenate(p_loc, axis=0))

        h_cen = h_c - segsum(h_c) * (1.0 / HEAD_DIM)
        yield
        h_var = segsum(h_cen * h_cen) * (1.0 / HEAD_DIM)
        zc = proj_s[HIST:HIST + TT, OFF_PC + 2 * D_C:OFF_PC + 3 * D_C]
        yc_out = _sigmoid(zc) * (h_cen * lax.rsqrt(h_var + GN_EPS_C) * vec(V_GN, D_C))
        yield
        ybc_s[:, D_B:D_B + D_C] = yc_out

    pending = [group_a(), group_b(), group_c()]
    while pending:
        pending = [g for g in pending if next(g, StopIteration) is not StopIteration]

    last3 = proj_s[HIST + TT - 3:HIST + TT, :]
    proj_s[HIST - 3:HIST, :] = last3
    ca_o[0] = last3[:, 0:D_A]
    sh_o[0] = last3[2:3, OFF_PB:OFF_PB + D_B_IN]
    cc_o[0] = last3[:, OFF_PC:OFF_PC + D_C]

    xo_ref[0] = (x_ref[0] + out_parts['a']) + jnp.dot(ybc_s[...].astype(BF16), w_out_ref[0, D_A:D_MODEL, :],
                                                      preferred_element_type=F32)


def _mixer_call(layer, x, states, wts, TT, L):
    B, T, _ = x.shape
    NT = T // TT

    def state_in_spec(shape):
        nd = len(shape)
        return pl.BlockSpec((None, 1) + tuple(shape[2:]), lambda b, j: (layer, b) + (0,) * (nd - 2))

    def state_out_spec(shape):
        nd = len(shape)
        return pl.BlockSpec((1,) + tuple(shape[1:]), lambda b, j: (b,) + (0,) * (nd - 1))

    def layer_spec(arr):
        nd = arr.ndim
        return pl.BlockSpec((1,) + tuple(arr.shape[1:]), lambda b, j: (layer,) + (0,) * (nd - 1))

    x_spec = pl.BlockSpec((1, TT, D_MODEL), lambda b, j: (b, j, 0))
    state_in_specs = [state_in_spec(s.shape) for s in states]
    state_out_specs = [state_out_spec(s.shape[1:]) for s in states]
    w_names = ('vec', 'w_in', 'lruw', 'w2', 'a2', 'g2', 'wqk', 'wif', 'w_out')
    w_arrs = [wts[n] for n in w_names]

    out_shape = ([jax.ShapeDtypeStruct(x.shape, F32)]
                 + [jax.ShapeDtypeStruct(s.shape[1:], F32) for s in states])
    scratch = [
        pltpu.VMEM((HIST + TT, D_IN), F32),
        pltpu.VMEM((7, TT, D_B), F32),
        pltpu.VMEM((2, TT, D_C), F32),
        pltpu.VMEM((TT, D_B), F32),
        pltpu.VMEM((TT, D_B + D_C), F32),
        pltpu.VMEM((N_HEAD_B, HEAD_DIM, HEAD_DIM), F32),
        pltpu.VMEM((N_PAIR_C, PAIR_W, PAIR_W), F32),
        pltpu.VMEM((1, D_C), F32),
        pltpu.VMEM((1, GATE_W), F32),
    ]
    outs = pl.pallas_call(
        functools.partial(_mixer_kernel, TT, L),
        grid=(B, NT),
        in_specs=[x_spec] + state_in_specs + [layer_spec(a) for a in w_arrs],
        out_specs=[x_spec] + state_out_specs,
        out_shape=out_shape,
        scratch_shapes=scratch,
        compiler_params=pltpu.CompilerParams(
            dimension_semantics=("arbitrary", "arbitrary"),
            vmem_limit_bytes=VMEM_LIMIT_MIXER),
        name="mixer",
    )(x, *states, *w_arrs)
    return outs[0], tuple(outs[1:])


FF_CHUNK = 768


def _ffn_kernel(final, x_ref, vec_ref, wi_ref, wo_ref, o_ref):
    x = x_ref[...]
    hn = _rms(x, vec_ref[0, 0:1, :]).astype(BF16)
    acc = x
    for lo in range(0, D_FF, FF_CHUNK):
        hi = min(lo + FF_CHUNK, D_FF)
        gate = jnp.dot(hn, wi_ref[0, :, lo:hi], preferred_element_type=F32)
        up = jnp.dot(hn, wi_ref[0, :, D_FF + lo:D_FF + hi], preferred_element_type=F32)
        act = (gate * _sigmoid(gate) * up).astype(BF16)
        acc = acc + jnp.dot(act, wo_ref[0, lo:hi, :], preferred_element_type=F32)
    if final:
        acc = _rms(acc, vec_ref[0, 1:2, :])
    o_ref[...] = acc


def _ffn_call(layer, x2d, wts, TM, final):
    M = x2d.shape[0]
    return pl.pallas_call(
        functools.partial(_ffn_kernel, final),
        grid=(M // TM,),
        in_specs=[
            pl.BlockSpec((TM, D_MODEL), lambda i: (i, 0)),
            pl.BlockSpec((1, 8, D_MODEL), lambda i: (layer, 0, 0)),
            pl.BlockSpec((1, D_MODEL, 2 * D_FF), lambda i: (layer, 0, 0),
                         pipeline_mode=pl.Buffered(1)),
            pl.BlockSpec((1, D_FF, D_MODEL), lambda i: (layer, 0, 0),
                         pipeline_mode=pl.Buffered(1)),
        ],
        out_specs=pl.BlockSpec((TM, D_MODEL), lambda i: (i, 0)),
        out_shape=jax.ShapeDtypeStruct(x2d.shape, F32),
        compiler_params=pltpu.CompilerParams(
            dimension_semantics=("arbitrary",),
            vmem_limit_bytes=VMEM_LIMIT_FFN),
        name="ffn",
    )(x2d, wts['ffn_vec'], wts['w_ffn_in'], wts['w_ffn_out'])


def _block_diag(w):
    P, n, d, e = w.shape
    eye = jnp.eye(n, dtype=w.dtype)
    return (w[:, :, :, None, :] * eye[None, :, None, :, None]).reshape(P, n * d, n * e)


def _pack_rows(P, rows):
    parts = []
    for r, arr in rows:
        arr = arr.reshape(P, -1, arr.shape[-1]).astype(F32)
        assert r == sum(p.shape[1] for p in parts)
        parts.append(jnp.pad(arr, ((0, 0), (0, 0), (0, VEC_W - arr.shape[2]))))
    used = sum(p.shape[1] for p in parts)
    parts.append(jnp.zeros((P, N_VEC - used, VEC_W), F32))
    return jnp.concatenate(parts, axis=1)


def _run_stream(x, states, wts, TT, L, TM):
    B, T, _ = x.shape
    new_states = []
    for l in range(DEPTH):
        x, st_new = _mixer_call(l, x, states, wts, TT, L)
        new_states.append(st_new)
        x = _ffn_call(l, x.reshape(B * T, D_MODEL), wts, TM, l == DEPTH - 1).reshape(B, T, D_MODEL)
    stacked = tuple(jnp.stack([st[i] for st in new_states]) for i in range(len(states)))
    return x, stacked


def kernel(x_prompt, x_sample, state_conv_a, state_lru, state_shift_b, state_wkv, state_conv_c, state_mem_c, state_mem_n, state_mem_m, norm1, w_in, conv_a_w, conv_a_b, lru_wr, lru_br, lru_wi, lru_bi, lru_lambda, norm_a, rwkv_mu, rwkv_w0, rwkv_w2, rwkv_a0, rwkv_a2, rwkv_g2, rwkv_kk, rwkv_ka, rwkv_rk, rwkv_lnw, rwkv_lnb, conv_c_w, conv_c_b, mlstm_wq, mlstm_wk, mlstm_wif, mlstm_bif, mlstm_gn, w_out, norm2, w_ffn_in, w_ffn_out, norm_f):
    P = DEPTH
    gate_pad = jnp.zeros((P, 3 * D_C, GATE_W - N_HEAD_C), F32)
    wif_pad = jnp.concatenate([mlstm_wif[:, :, 0:N_HEAD_C], gate_pad,
                               mlstm_wif[:, :, N_HEAD_C:], gate_pad], axis=-1)
    wts = {
        'vec': _pack_rows(P, [
            (V_NORM1, norm1), (V_CONV_A_B, conv_a_b),
            (V_LRU_B, jnp.concatenate([lru_br, lru_bi], axis=-1)), (V_LAM, lru_lambda),
            (V_NORM_A, norm_a), (V_MU, rwkv_mu), (V_W0, rwkv_w0), (V_A0, rwkv_a0),
            (V_KK, rwkv_kk), (V_KA, rwkv_ka), (V_RK, rwkv_rk.reshape(P, D_B)),
            (V_LNW, rwkv_lnw), (V_LNB, rwkv_lnb), (V_CONV_C_B, conv_c_b),
            (V_BIF, mlstm_bif[:, 0:N_HEAD_C]), (V_GN, mlstm_gn),
            (V_CONV_A_W, conv_a_w), (V_CONV_C_W, conv_c_w), (V_BIF2, mlstm_bif[:, N_HEAD_C:])]),
        'w_in': w_in.astype(BF16),
        'lruw': jnp.concatenate([_block_diag(lru_wr), _block_diag(lru_wi)], axis=-1).astype(BF16),
        'w2': rwkv_w2.astype(BF16),
        'a2': rwkv_a2.astype(BF16),
        'g2': rwkv_g2.astype(BF16),
        'wqk': jnp.concatenate([_block_diag(mlstm_wq), _block_diag(mlstm_wk)], axis=-1).astype(BF16),
        'wif': wif_pad.astype(BF16),
        'w_out': w_out.astype(BF16),
        'ffn_vec': jnp.concatenate([norm2[:, None, :],
                                    jnp.broadcast_to(norm_f[None, None, :], (P, 1, D_MODEL)),
                                    jnp.zeros((P, 6, D_MODEL), F32)], axis=1),
        'w_ffn_in': w_ffn_in.astype(BF16),
        'w_ffn_out': w_ffn_out.astype(BF16),
    }

    def to_kernel_layout(conv_a, lru, shift, wkv, conv_c, mem_c, mem_n, mem_m):
        Bn = lru.shape[1]
        return (conv_a, lru.reshape(P, Bn, 1, D_A), shift, wkv, conv_c,
                mem_c, mem_n, mem_m.reshape(P, Bn, 1, N_HEAD_C))

    def from_kernel_layout(conv_a, lru, shift, wkv, conv_c, mem_c, mem_n, mem_m):
        Bn = lru.shape[1]
        return (conv_a, lru.reshape(P, Bn, D_A), shift, wkv, conv_c,
                mem_c, mem_n, mem_m.reshape(P, Bn, N_HEAD_C))

    Bp, Tp, _ = x_prompt.shape
    Bs, Ts, _ = x_sample.shape
    zero_states = (jnp.zeros((P, Bp, CONV_W - 1, D_A), F32),
                   jnp.zeros((P, Bp, D_A), F32),
                   jnp.zeros((P, Bp, 1, D_B_IN), F32),
                   jnp.zeros((P, Bp, N_HEAD_B, HEAD_DIM, HEAD_DIM), F32),
                   jnp.zeros((P, Bp, CONV_W - 1, D_C), F32),
                   jnp.zeros((P, Bp, N_HEAD_C, HEAD_DIM, HEAD_DIM), F32),
                   jnp.zeros((P, Bp, N_HEAD_C, HEAD_DIM), F32),
                   jnp.zeros((P, Bp, N_HEAD_C), F32))
    Lp = MLSTM_CHUNK if Tp % MLSTM_CHUNK == 0 else Tp
    Ls = MLSTM_CHUNK if Ts % MLSTM_CHUNK == 0 else Ts
    TTp = 256 if Tp % 256 == 0 else Lp
    y_prompt, p_states = _run_stream(x_prompt, to_kernel_layout(*zero_states), wts, TTp, Lp, 512)
    y_sample, s_states = _run_stream(
        x_sample, to_kernel_layout(state_conv_a, state_lru, state_shift_b, state_wkv,
                                   state_conv_c, state_mem_c, state_mem_n, state_mem_m),
        wts, Ls, Ls, Bs * Ts)
    return (y_prompt, y_sample) + from_kernel_layout(*p_states) + from_kernel_layout(*s_states)
```

```python
import functools
import math

import jax
import jax.numpy as jnp
from jax import lax
from jax.experimental import pallas as pl
from jax.experimental.pallas import tpu as pltpu

F32 = jnp.float32
BF16 = jnp.bfloat16

D_MODEL = 1024
DEPTH = 4
HEAD_DIM = 64
D_A = 256
N_BLK_A = 4
CONV_W = 4
LRU_C = 8.0
D_B = 384
N_HEAD_B = 6
LORA_W = 64
LORA_A = 64
LORA_G = 128
D_B_IN = 3 * D_B + LORA_W + LORA_A + LORA_G
D_C = 384
N_HEAD_C = 6
D_IN = 2 * D_A + D_B_IN + 3 * D_C
D_FF = 2816
RMS_EPS = 1e-6
GN_EPS_B = 64e-5
GN_EPS_C = 1e-6
MLSTM_CHUNK = 64

OFF_PB = 2 * D_A
OFF_PC = OFF_PB + D_B_IN
IN_SPLITS = ((0, OFF_PB), (OFF_PB, 2048), (2048, D_IN))
HIST = 8
GATE_W = 128
PAIR_W = 2 * HEAD_DIM
N_PAIR_C = N_HEAD_C // 2

(V_NORM1, V_CONV_A_B, V_LRU_B, V_LAM, V_NORM_A, V_MU, V_W0, V_A0, V_KK, V_KA, V_RK,
 V_LNW, V_LNB, V_CONV_C_B, V_BIF, V_GN, V_CONV_A_W, V_CONV_C_W, V_BIF2) = (
     0, 1, 2, 3, 4, 5, 6, 7, 8, 9, 10, 11, 12, 13, 14, 15, 16, 20, 24)
N_VEC = 32
VEC_W = D_B_IN

VMEM_LIMIT_MIXER = 48 * 1024 * 1024
VMEM_LIMIT_FFN = 56 * 1024 * 1024


def _bdot(a, b):
    return jnp.dot(a.astype(BF16), b.astype(BF16), preferred_element_type=F32)


def _bdot_nt(a, b):
    return lax.dot_general(a.astype(BF16), b.astype(BF16), (((1,), (1,)), ((), ())),
                           preferred_element_type=F32)


def _split(a, n):
    parts = []
    rem = a
    for i in range(n):
        p = rem.astype(BF16)
        parts.append(p)
        if i + 1 < n:
            rem = rem - p.astype(F32)
    return parts


def _edot_r(a, e, n=3):
    acc = None
    for p in _split(a, n):
        t = jnp.dot(p, e, preferred_element_type=F32)
        acc = t if acc is None else acc + t
    return acc


def _edot_l(e, a, n=3):
    acc = None
    for p in _split(a, n):
        t = jnp.dot(e, p, preferred_element_type=F32)
        acc = t if acc is None else acc + t
    return acc


def _xdot(a, b, nt=False):
    return _bdot_nt(a, b) if nt else _bdot(a, b)


def _softplus(z):
    return jnp.maximum(z, 0.0) + jnp.log(1.0 + jnp.exp(-jnp.abs(z)))


def _sigmoid(z):
    return 0.5 * jnp.tanh(0.5 * z) + 0.5


def _rms(x, g):
    return x * lax.rsqrt(jnp.mean(x * x, axis=-1, keepdims=True) + RMS_EPS) * g


def _iota2(shape, dim):
    return lax.broadcasted_iota(jnp.int32, shape, dim)


def _mixer_kernel(TT, L,
                  x_ref, ca_ref, lru_ref, sh_ref, wkv_ref, cc_ref, mc_ref, mn_ref, mm_ref,
                  vec_ref, w_in_ref, lruw_ref, w2_ref, a2_ref, g2_ref, wqk_ref, wif_ref, w_out_ref,
                  xo_ref, ca_o, lru_o, sh_o, wkv_o, cc_o, mc_o, mn_o, mm_o,
                  proj_s, rk_s, ml_s, yb_s, ybc_s, wkv_s, cbd_s, nrow_s, mrow_s):
    j = pl.program_id(1)
    NC = TT // L
    LOG_L = int(math.log2(L))
    CH = range(NC)
    HB = range(N_HEAD_B)
    sls = [slice(h * HEAD_DIM, (h + 1) * HEAD_DIM) for h in HB]
    rws = [slice(c * L, (c + 1) * L) for c in CH]

    def vec(row, width):
        return vec_ref[0, row:row + 1, 0:width]

    @pl.when(j == 0)
    def _init():
        proj_s[0:HIST, :] = jnp.zeros((HIST, D_IN), F32)
        proj_s[HIST - 3:HIST, 0:D_A] = ca_ref[0]
        proj_s[HIST - 1:HIST, OFF_PB:OFF_PB + D_B_IN] = sh_ref[0]
        proj_s[HIST - 3:HIST, OFF_PC:OFF_PC + D_C] = cc_ref[0]
        lru_o[...] = lru_ref[...]
        for h in HB:
            wkv_s[:, sls[h]] = wkv_ref[0, h].T
        cbd_s[...] = jnp.zeros((N_PAIR_C, PAIR_W, PAIR_W), F32)
        mrow_s[...] = jnp.zeros((1, GATE_W), F32)
        for h in range(N_HEAD_C):
            cbd_s[h // 2, sls[h % 2], sls[h % 2]] = mc_ref[0, h]
            nrow_s[:, sls[h]] = mn_ref[0, h:h + 1, :]
        mrow_s[:, 0:N_HEAD_C] = mm_ref[0]

    x = x_ref[0]
    hn = _rms(x, vec(V_NORM1, D_MODEL))
    hn_b = hn.astype(BF16)
    for lo, hi in IN_SPLITS:
        proj_s[HIST:HIST + TT, lo:hi] = jnp.dot(hn_b, w_in_ref[0, :, lo:hi],
                                                preferred_element_type=F32)
    out_parts = {}

    def taps(col0, width, w_row0):
        acc = None
        for t in range(CONV_W):
            r0 = HIST - (CONV_W - 1) + t
            term = vec(w_row0 + t, width) * proj_s[r0:r0 + TT, col0:col0 + width]
            acc = term if acc is None else acc + term
        return acc

    row_t = _iota2((TT, 1), 0)

    seg_r = _iota2((D_B, D_B), 0) >> 6
    seg_c = _iota2((D_B, D_B), 1) >> 6
    ones_bd = jnp.where(seg_r == seg_c, 1.0, 0.0).astype(BF16)
    seg_ones = jnp.where((_iota2((D_C, GATE_W), 0) >> 6) == _iota2((D_C, GATE_W), 1),
                         1.0, 0.0).astype(BF16)
    expand_m = jnp.where(_iota2((GATE_W, D_C), 0) == (_iota2((GATE_W, D_C), 1) >> 6),
                         1.0, 0.0).astype(BF16)
    tt_r = _iota2((TT, TT), 0)
    tt_c = _iota2((TT, TT), 1)
    same_chunk = (tt_r >> LOG_L) == (tt_c >> LOG_L)
    blk_tril = jnp.where(same_chunk & (tt_r >= tt_c), 1.0, 0.0).astype(BF16)

    def segsum(z):
        return _edot_r(z, ones_bd, n=1)

    def expand(z, n=2):
        return _edot_r(z, expand_m, n=n)

    def group_a():
        xa = taps(0, D_A, V_CONV_A_W) + vec(V_CONV_A_B, D_A)
        gates_a = _bdot(xa, lruw_ref[0]) + vec(V_LRU_B, 2 * D_A)
        gate_r = _sigmoid(gates_a[:, 0:D_A])
        gate_i = _sigmoid(gates_a[:, D_A:2 * D_A])
        log_a = (-LRU_C) * gate_r * _softplus(-vec(V_LAM, D_A))
        a_cum = jnp.exp(log_a)
        one_m_a2 = 1.0 - jnp.exp(2.0 * log_a)
        h_loc = jnp.where(one_m_a2 > 0.0, one_m_a2 * lax.rsqrt(one_m_a2), 0.0) * (gate_i * xa)
        yield
        d = 1
        while d < TT:
            keep = row_t >= d
            a_sh = jnp.where(keep, pltpu.roll(a_cum, d, 0), 1.0)
            h_sh = jnp.where(keep, pltpu.roll(h_loc, d, 0), 0.0)
            h_loc = a_cum * h_sh + h_loc
            a_cum = a_cum * a_sh
            d *= 2
            yield
        h_lru = h_loc + a_cum * lru_o[0]
        lru_o[0] = h_lru[TT - 1:TT, :]
        pag = proj_s[HIST:HIST + TT, D_A:2 * D_A]
        gelu = 0.5 * pag * (1.0 + jnp.tanh(math.sqrt(2.0 / math.pi)
                                           * (pag + 0.044715 * (pag * pag * pag))))
        ya = _rms(h_lru, vec(V_NORM_A, D_A)) * gelu
        yield
        out_parts['a'] = jnp.dot(ya.astype(BF16), w_out_ref[0, 0:D_A, :],
                                 preferred_element_type=F32)

    def group_b():
        pb = proj_s[HIST:HIST + TT, OFF_PB:OFF_PB + D_B_IN]
        prev = proj_s[HIST - 1:HIST - 1 + TT, OFF_PB:OFF_PB + D_B_IN]
        xs = pb + (prev - pb) * vec(V_MU, D_B_IN)
        r_b = xs[:, 0:D_B]
        k_b = xs[:, D_B:2 * D_B]
        v_b = xs[:, 2 * D_B:3 * D_B]
        o3 = 3 * D_B
        wd = xs[:, o3:o3 + LORA_W]
        ad = xs[:, o3 + LORA_W:o3 + LORA_W + LORA_A]
        gd = xs[:, o3 + LORA_W + LORA_A:D_B_IN]
        w_log = -_softplus(-(vec(V_W0, D_B) + _bdot(jnp.tanh(wd), w2_ref[0]))) - 0.5
        logw = -jnp.exp(w_log)
        a_b = _sigmoid(vec(V_A0, D_B) + _bdot(ad, a2_ref[0]))
        g_b = _bdot(_sigmoid(gd), g2_ref[0])
        yield
        kk = k_b * vec(V_KK, D_B)
        kk = kk * lax.rsqrt(jnp.maximum(segsum(kk * kk), 1e-24))
        k_b = k_b * (1.0 + (a_b - 1.0) * vec(V_KA, D_B))
        bonus = segsum(r_b * k_b * vec(V_RK, D_B)) * v_b
        ka_b = kk * a_b
        yield
        cl = _edot_l(blk_tril, logw)
        cl_last = [cl[c * L + L - 1:c * L + L, :] for c in CH]
        ctot = jnp.concatenate([jnp.broadcast_to(cl_last[c], (L, D_B)) for c in CH], axis=0)
        w_inv = jnp.exp(-cl)
        to_end = jnp.exp(ctot - cl)
        rk_s[0] = kk * jnp.exp(cl - logw)
        rk_s[1] = k_b * w_inv
        rk_s[2] = ka_b * w_inv
        rk_s[3] = r_b * jnp.exp(cl)
        rk_s[4] = v_b
        rk_s[5] = k_b * to_end
        rk_s[6] = ka_b * to_end
        w_end = [jnp.exp(cl_last[c]) for c in CH]

        yield
        GROUPS = ((0, 4), (4, 2))

        def group_consts(nh):
            nl, w = nh * L, nh * HEAD_DIM
            t_l = _iota2((L, nl), 0)
            s_l = _iota2((L, nl), 1) & (L - 1)
            return dict(
                nat=jnp.where((_iota2((nl, w), 0) >> LOG_L) == (_iota2((nl, w), 1) >> 6),
                              1.0, 0.0).astype(BF16),
                sq=jnp.where((_iota2((nl, nl), 0) >> LOG_L) == (_iota2((nl, nl), 1) >> LOG_L),
                             1.0, 0.0).astype(BF16),
                t=t_l, s=s_l, strict=t_l > s_l, incl=t_l >= s_l,
                eye=jnp.where(t_l == s_l, 1.0, 0.0))

        gconst = {nh: group_consts(nh) for _, nh in GROUPS}

        def bdiag(z, mask):
            zb = z.astype(BF16)
            return jnp.concatenate([zb] * (mask.shape[0] // L), axis=0) * mask

        def mm(a_, b_bf16):
            return jnp.dot(a_.astype(BF16), b_bf16, preferred_element_type=F32)

        def mm_nt(a_, b_bf16):
            return lax.dot_general(a_.astype(BF16), b_bf16, (((1,), (1,)), ((), ())),
                                   preferred_element_type=F32)

        CG = [(c, g) for c in CH for g in range(len(GROUPS))]

        def gslice(i, c, g):
            h0, nh = GROUPS[g]
            return rk_s[i, rws[c], h0 * HEAD_DIM:(h0 + nh) * HEAD_DIM]

        def gc(g):
            return gconst[GROUPS[g][1]]

        qg = {p: gslice(0, *p) for p in CG}
        rg = {p: gslice(3, *p) for p in CG}
        vg = {p: gslice(4, *p) for p in CG}
        qr = {p: jnp.concatenate([qg[p], rg[p]], axis=0) for p in CG}
        sa = {(c, g): mm_nt(qr[c, g], bdiag(gslice(2, c, g), gc(g)['nat'])) for c, g in CG}
        sk = {(c, g): mm_nt(qr[c, g], bdiag(gslice(1, c, g), gc(g)['nat'])) for c, g in CG}
        n_m = {(c, g): jnp.where(gc(g)['strict'], sa[c, g][0:L], 0.0) for c, g in CG}
        mra_m = {(c, g): jnp.where(gc(g)['incl'], sa[c, g][L:2 * L], 0.0) for c, g in CG}
        mk2 = {(c, g): jnp.concatenate([jnp.where(gc(g)['strict'], sk[c, g][0:L], 0.0),
                                        jnp.where(gc(g)['incl'], sk[c, g][L:2 * L], 0.0)], axis=0)
               for c, g in CG}
        yield
        bd_v = {(c, g): bdiag(vg[c, g], gc(g)['nat']) for c, g in CG}
        mv2 = {p: mm(mk2[p], bd_v[p]) for p in CG}
        mkv = {p: mv2[p][0:L] for p in CG}
        mrkv = {p: mv2[p][L:2 * L] for p in CG}
        yield
        xinv = {(c, g): gc(g)['eye'] - jnp.where(((gc(g)['t'] & 1) == 1) & (gc(g)['s'] == gc(g)['t'] - 1),
                                                  n_m[c, g], 0.0) for c, g in CG}
        for lg in range(1, LOG_L):
            lvl = {nh: (((k['t'] >> lg) & 1) == 1) & ((k['s'] >> lg) == (k['t'] >> lg) - 1)
                   for nh, k in gconst.items()}
            xc = {(c, g): mm(xinv[c, g], bdiag(jnp.where(lvl[GROUPS[g][1]], n_m[c, g], 0.0), gc(g)['sq']))
                  for c, g in CG}
            yield
            xinv = {(c, g): xinv[c, g] - mm(xc[c, g], bdiag(xinv[c, g], gc(g)['sq'])) for c, g in CG}
            yield
        p1 = {(c, g): mm(xinv[c, g], bdiag(qg[c, g], gc(g)['nat'])) for c, g in CG}
        p2 = {(c, g): mm(xinv[c, g], bdiag(mkv[c, g], gc(g)['nat'])) for c, g in CG}
        yield
        bd_p1 = {(c, g): bdiag(p1[c, g], gc(g)['nat']) for c, g in CG}
        bd_p2 = {(c, g): bdiag(p2[c, g], gc(g)['nat']) for c, g in CG}
        rq = {p: rg[p] - mm(mra_m[p], bd_p1[p]) for p in CG}
        yc = {p: mrkv[p] - mm(mra_m[p], bd_p2[p]) for p in CG}

        yield
        def head_cat(z, nh):
            return jnp.concatenate([z[i * HEAD_DIM:(i + 1) * HEAD_DIM, :] for i in range(nh)], axis=1)

        def glanes(g):
            h0, nh = GROUPS[g]
            return slice(h0 * HEAD_DIM, (h0 + nh) * HEAD_DIM)

        sq64 = {nh: jnp.where((_iota2((nh * HEAD_DIM, nh * HEAD_DIM), 0) >> 6)
                              == (_iota2((nh * HEAD_DIM, nh * HEAD_DIM), 1) >> 6), 1.0, 0.0).astype(BF16)
                for _, nh in GROUPS}
        at_cat = {(c, g): head_cat(rk_s[6, rws[c], glanes(g)].T, GROUPS[g][1]) for c, g in CG}
        kt_cat = {(c, g): head_cat(rk_s[5, rws[c], glanes(g)].T, GROUPS[g][1]) for c, g in CG}
        wc_cat = {(c, g): head_cat(jnp.broadcast_to(w_end[c][:, glanes(g)],
                                                    (HEAD_DIM, GROUPS[g][1] * HEAD_DIM)).T, GROUPS[g][1])
                  for c, g in CG}
        yield
        g_cat = {p: mm(at_cat[p], bd_p1[p]) for p in CG}
        h_cat = {p: mm(kt_cat[p], bd_v[p]) - mm(at_cat[p], bd_p2[p]) for p in CG}
        yield
        NG = range(len(GROUPS))
        s_cat = [wkv_s[:, glanes(g)] for g in NG]
        for c in CH:
            for g in NG:
                nh = GROUPS[g][1]
                bd_s = jnp.concatenate([s_cat[g].astype(BF16)] * nh, axis=0) * sq64[nh]
                ys = mm(jnp.concatenate([rq[c, g], g_cat[c, g]], axis=0), bd_s)
                yb_s[rws[c], glanes(g)] = ys[0:L] + yc[c, g]
                s_cat[g] = wc_cat[c, g] * s_cat[g] - ys[L:L + HEAD_DIM] + h_cat[c, g]
            yield
        for g in NG:
            h0, nh = GROUPS[g]
            wkv_s[:, glanes(g)] = s_cat[g]
            s_tr = s_cat[g].T
            for i in range(nh):
                wkv_o[0, h0 + i] = s_tr[i * HEAD_DIM:(i + 1) * HEAD_DIM, :]

        y_b = yb_s[...]
        y_cen = y_b - segsum(y_b) * (1.0 / HEAD_DIM)
        yield
        y_var = segsum(y_cen * y_cen) * (1.0 / HEAD_DIM)
        y_n = y_cen * lax.rsqrt(y_var + GN_EPS_B) * vec(V_LNW, D_B) + vec(V_LNB, D_B)
        yb_out = (y_n + bonus) * g_b
        yield
        ybc_s[:, 0:D_B] = yb_out

    def group_c():
        xconv = taps(OFF_PC, D_C, V_CONV_C_W) + vec(V_CONV_C_B, D_C)
        xact = xconv * _sigmoid(xconv)
        qk = _bdot(xact, wqk_ref[0])
        q_c = qk[:, 0:D_C]
        k_c = qk[:, D_C:2 * D_C]
        vc = proj_s[HIST:HIST + TT, OFF_PC + D_C:OFF_PC + 2 * D_C]
        gates_c = (_bdot(q_c, wif_ref[0, 0:D_C, :]) + _bdot(k_c, wif_ref[0, D_C:2 * D_C, :])
                   + _bdot(vc, wif_ref[0, 2 * D_C:3 * D_C, :]))
        i_pre = gates_c[:, 0:GATE_W] + vec(V_BIF, GATE_W)
        logf = -_softplus(-(gates_c[:, GATE_W:2 * GATE_W] + vec(V_BIF2, GATE_W)))
        ml_s[0] = q_c
        ml_s[1] = k_c * (HEAD_DIM ** -0.5)

        yield
        b_col = _edot_l(blk_tril, logf)
        yield
        g_col = i_pre - b_col
        g_max = g_col
        pos = row_t & (L - 1)
        d = 1
        while d < L:
            g_max = jnp.where(pos >= d, jnp.maximum(g_max, pltpu.roll(g_max, d, 0)), g_max)
            d *= 2
        yield
        m_loc = b_col + g_max
        g_last = [g_max[c * L + L - 1:c * L + L, :] for c in CH]
        b_end = [b_col[c * L + L - 1:c * L + L, :] for c in CH]
        m_loc_end = [b_end[c] + g_last[c] for c in CH]
        w_end_loc = jnp.concatenate([jnp.exp(g_col[rws[c], :] - g_last[c]) for c in CH], axis=0)
        kw = ml_s[1] * expand(w_end_loc, n=1)
        yield
        s_of_lane = _iota2((L, N_HEAD_C * L), 1) & (L - 1)
        t_of_row = _iota2((L, N_HEAD_C * L), 0)
        causal_t = t_of_row >= s_of_lane
        eye_t = t_of_row == s_of_lane
        bd_l = (_iota2((N_HEAD_C * L, D_C), 0) >> LOG_L) == (_iota2((N_HEAD_C * L, D_C), 1) >> 6)
        seg_ones_l = jnp.where((_iota2((N_HEAD_C * L, GATE_W), 0) >> LOG_L)
                               == _iota2((N_HEAD_C * L, GATE_W), 1), 1.0, 0.0).astype(BF16)
        expand_l = jnp.where(_iota2((GATE_W, N_HEAD_C * L), 0)
                             == (_iota2((GATE_W, N_HEAD_C * L), 1) >> LOG_L), 1.0, 0.0).astype(BF16)
        g_exp_l = _edot_r(g_col, expand_l, n=2)
        g_max_exp_l = _edot_r(g_max, expand_l, n=2)

        pws = [slice(p * PAIR_W, (p + 1) * PAIR_W) for p in range(N_PAIR_C)]
        pair_mask = (_iota2((PAIR_W, PAIR_W), 0) >> 6) == (_iota2((PAIR_W, PAIR_W), 1) >> 6)
        yield
        p_loc, den_loc, c_loc, n_loc = [], [], [], []
        for c in CH:
            q_cc = ml_s[0, rws[c], :]
            k_cc = ml_s[1, rws[c], :]
            v_cc = proj_s[HIST + c * L:HIST + (c + 1) * L, OFF_PC + D_C:OFF_PC + 2 * D_C]
            k_bd = jnp.where(bd_l, jnp.concatenate([k_cc] * N_HEAD_C, axis=0), 0.0)
            v_bd = jnp.where(bd_l, jnp.concatenate([v_cc] * N_HEAD_C, axis=0), 0.0)
            g_flat = jnp.sum(jnp.where(eye_t, g_exp_l[rws[c], :], 0.0), axis=0, keepdims=True)
            dmat = jnp.where(causal_t, jnp.exp(jnp.minimum(g_flat - g_max_exp_l[rws[c], :], 0.0)), 0.0)
            scores = _bdot_nt(q_cc, k_bd) * dmat
            p_loc.append(_bdot(scores, v_bd))
            den_loc.append(_edot_r(scores, seg_ones_l, n=1))
            kw_c = kw[rws[c], :]
            kw_t = kw_c.T
            c_loc.append([jnp.where(pair_mask, _bdot(kw_t[pws[p], :], v_cc[:, pws[p]]), 0.0)
                          for p in range(N_PAIR_C)])
            n_loc.append(jnp.sum(kw_c, axis=0, keepdims=True))
            yield

        m_in = []
        m_cur = mrow_s[...]
        for c in CH:
            m_in.append(m_cur)
            m_cur = jnp.maximum(b_end[c] + m_cur, m_loc_end[c])
        m_out = m_in[1:] + [m_cur]
        mrow_s[...] = m_cur
        scale_rows = jnp.concatenate(
            [jnp.exp(b_end[c] + m_in[c] - m_out[c]) for c in CH]
            + [jnp.exp(m_loc_end[c] - m_out[c]) for c in CH], axis=0)
        scale_exp = expand(scale_rows)
        yield
        c_in, n_in = [], []
        c_cur = [cbd_s[p] for p in range(N_PAIR_C)]
        n_cur = nrow_s[...]
        for c in CH:
            c_in.append(c_cur)
            n_in.append(n_cur)
            w0 = scale_exp[c:c + 1, :]
            wl = scale_exp[NC + c:NC + c + 1, :]
            c_cur = [c_cur[p] * w0[:, pws[p]] + c_loc[c][p] * wl[:, pws[p]] for p in range(N_PAIR_C)]
            n_cur = n_cur * w0 + n_loc[c] * wl
        for p in range(N_PAIR_C):
            cbd_s[p] = c_cur[p]
        nrow_s[...] = n_cur
        for h in range(N_HEAD_C):
            mc_o[0, h] = c_cur[h // 2][sls[h % 2], sls[h % 2]]
            mn_o[0, h:h + 1, :] = n_cur[:, sls[h]]
        mm_o[0] = m_cur[:, 0:N_HEAD_C]

        yield
        m0_b = jnp.concatenate([jnp.broadcast_to(m_in[c], (L, GATE_W)) for c in CH], axis=0)
        log_inter = b_col + m0_b
        m_t = jnp.maximum(log_inter, m_loc)
        w_inter = jnp.exp(log_inter - m_t)
        w_local = jnp.exp(m_loc - m_t)
        q_all = ml_s[0]
        qc0 = jnp.concatenate(
            [jnp.concatenate([_bdot(q_all[rws[c], pws[p]], c_in[c][p]) for p in range(N_PAIR_C)], axis=1)
             for c in CH], axis=0)
        qn0 = jnp.concatenate([_edot_r(q_all[rws[c], :] * n_in[c], seg_ones, n=1) for c in CH], axis=0)
        yield
        den = w_inter * qn0 + w_local * jnp.concatenate(den_loc, axis=0)
        inv_den = 1.0 / jnp.maximum(jnp.abs(den), jnp.exp(-m_t))
        h_c = (expand(w_inter * inv_den) * qc0
               + expand(w_local * inv_den) * jnp.concatenate(p_loc, axis=0))

        h_cen = h_c - segsum(h_c) * (1.0 / HEAD_DIM)
        yield
        h_var = segsum(h_cen * h_cen) * (1.0 / HEAD_DIM)
        zc = proj_s[HIST:HIST + TT, OFF_PC + 2 * D_C:OFF_PC + 3 * D_C]
        yc_out = _sigmoid(zc) * (h_cen * lax.rsqrt(h_var + GN_EPS_C) * vec(V_GN, D_C))
        yield
        ybc_s[:, D_B:D_B + D_C] = yc_out

    pending = [group_a(), group_b(), group_c()]
    while pending:
        pending = [g for g in pending if next(g, StopIteration) is not StopIteration]

    last3 = proj_s[HIST + TT - 3:HIST + TT, :]
    proj_s[HIST - 3:HIST, :] = last3
    ca_o[0] = last3[:, 0:D_A]
    sh_o[0] = last3[2:3, OFF_PB:OFF_PB + D_B_IN]
    cc_o[0] = last3[:, OFF_PC:OFF_PC + D_C]

    xo_ref[0] = (x_ref[0] + out_parts['a']) + jnp.dot(ybc_s[...].astype(BF16), w_out_ref[0, D_A:D_MODEL, :],
                                                      preferred_element_type=F32)


def _mixer_call(layer, x, states, wts, TT, L):
    B, T, _ = x.shape
    NT = T // TT

    def state_in_spec(shape):
        nd = len(shape)
        return pl.BlockSpec((None, 1) + tuple(shape[2:]), lambda b, j: (layer, b) + (0,) * (nd - 2))

    def state_out_spec(shape):
        nd = len(shape)
        return pl.BlockSpec((1,) + tuple(shape[1:]), lambda b, j: (b,) + (0,) * (nd - 1))

    def layer_spec(arr):
        nd = arr.ndim
        return pl.BlockSpec((1,) + tuple(arr.shape[1:]), lambda b, j: (layer,) + (0,) * (nd - 1))

    x_spec = pl.BlockSpec((1, TT, D_MODEL), lambda b, j: (b, j, 0))
    state_in_specs = [state_in_spec(s.shape) for s in states]
    state_out_specs = [state_out_spec(s.shape[1:]) for s in states]
    w_names = ('vec', 'w_in', 'lruw', 'w2', 'a2', 'g2', 'wqk', 'wif', 'w_out')
    w_arrs = [wts[n] for n in w_names]

    out_shape = ([jax.ShapeDtypeStruct(x.shape, F32)]
                 + [jax.ShapeDtypeStruct(s.shape[1:], F32) for s in states])
    scratch = [
        pltpu.VMEM((HIST + TT, D_IN), F32),
        pltpu.VMEM((7, TT, D_B), F32),
        pltpu.VMEM((2, TT, D_C), F32),
        pltpu.VMEM((TT, D_B), F32),
        pltpu.VMEM((TT, D_B + D_C), F32),
        pltpu.VMEM((HEAD_DIM, D_B), F32),
        pltpu.VMEM((N_PAIR_C, PAIR_W, PAIR_W), F32),
        pltpu.VMEM((1, D_C), F32),
        pltpu.VMEM((1, GATE_W), F32),
    ]
    outs = pl.pallas_call(
        functools.partial(_mixer_kernel, TT, L),
        grid=(B, NT),
        in_specs=[x_spec] + state_in_specs + [layer_spec(a) for a in w_arrs],
        out_specs=[x_spec] + state_out_specs,
        out_shape=out_shape,
        scratch_shapes=scratch,
        compiler_params=pltpu.CompilerParams(
            dimension_semantics=("arbitrary", "arbitrary"),
            vmem_limit_bytes=VMEM_LIMIT_MIXER),
        name="mixer",
    )(x, *states, *w_arrs)
    return outs[0], tuple(outs[1:])


FF_CHUNK = 768


def _ffn_kernel(final, x_ref, vec_ref, wi_ref, wo_ref, o_ref):
    x = x_ref[...]
    hn = _rms(x, vec_ref[0, 0:1, :]).astype(BF16)
    acc = x
    for lo in range(0, D_FF, FF_CHUNK):
        hi = min(lo + FF_CHUNK, D_FF)
        gate = jnp.dot(hn, wi_ref[0, :, lo:hi], preferred_element_type=F32)
        up = jnp.dot(hn, wi_ref[0, :, D_FF + lo:D_FF + hi], preferred_element_type=F32)
        act = (gate * _sigmoid(gate) * up).astype(BF16)
        acc = acc + jnp.dot(act, wo_ref[0, lo:hi, :], preferred_element_type=F32)
    if final:
        acc = _rms(acc, vec_ref[0, 1:2, :])
    o_ref[...] = acc


def _ffn_call(layer, x2d, wts, TM, final):
    M = x2d.shape[0]
    return pl.pallas_call(
        functools.partial(_ffn_kernel, final),
        grid=(M // TM,),
        in_specs=[
            pl.BlockSpec((TM, D_MODEL), lambda i: (i, 0)),
            pl.BlockSpec((1, 8, D_MODEL), lambda i: (layer, 0, 0)),
            pl.BlockSpec((1, D_MODEL, 2 * D_FF), lambda i: (layer, 0, 0),
                         pipeline_mode=pl.Buffered(1)),
            pl.BlockSpec((1, D_FF, D_MODEL), lambda i: (layer, 0, 0),
                         pipeline_mode=pl.Buffered(1)),
        ],
        out_specs=pl.BlockSpec((TM, D_MODEL), lambda i: (i, 0)),
        out_shape=jax.ShapeDtypeStruct(x2d.shape, F32),
        compiler_params=pltpu.CompilerParams(
            dimension_semantics=("arbitrary",),
            vmem_limit_bytes=VMEM_LIMIT_FFN),
        name="ffn",
    )(x2d, wts['ffn_vec'], wts['w_ffn_in'], wts['w_ffn_out'])


def _block_diag(w):
    P, n, d, e = w.shape
    eye = jnp.eye(n, dtype=w.dtype)
    return (w[:, :, :, None, :] * eye[None, :, None, :, None]).reshape(P, n * d, n * e)


def _pack_rows(P, rows):
    parts = []
    for r, arr in rows:
        arr = arr.reshape(P, -1, arr.shape[-1]).astype(F32)
        assert r == sum(p.shape[1] for p in parts)
        parts.append(jnp.pad(arr, ((0, 0), (0, 0), (0, VEC_W - arr.shape[2]))))
    used = sum(p.shape[1] for p in parts)
    parts.append(jnp.zeros((P, N_VEC - used, VEC_W), F32))
    return jnp.concatenate(parts, axis=1)


def _run_stream(x, states, wts, TT, L, TM):
    B, T, _ = x.shape
    new_states = []
    for l in range(DEPTH):
        x, st_new = _mixer_call(l, x, states, wts, TT, L)
        new_states.append(st_new)
        x = _ffn_call(l, x.reshape(B * T, D_MODEL), wts, TM, l == DEPTH - 1).reshape(B, T, D_MODEL)
    stacked = tuple(jnp.stack([st[i] for st in new_states]) for i in range(len(states)))
    return x, stacked


def kernel(x_prompt, x_sample, state_conv_a, state_lru, state_shift_b, state_wkv, state_conv_c, state_mem_c, state_mem_n, state_mem_m, norm1, w_in, conv_a_w, conv_a_b, lru_wr, lru_br, lru_wi, lru_bi, lru_lambda, norm_a, rwkv_mu, rwkv_w0, rwkv_w2, rwkv_a0, rwkv_a2, rwkv_g2, rwkv_kk, rwkv_ka, rwkv_rk, rwkv_lnw, rwkv_lnb, conv_c_w, conv_c_b, mlstm_wq, mlstm_wk, mlstm_wif, mlstm_bif, mlstm_gn, w_out, norm2, w_ffn_in, w_ffn_out, norm_f):
    P = DEPTH
    gate_pad = jnp.zeros((P, 3 * D_C, GATE_W - N_HEAD_C), F32)
    wif_pad = jnp.concatenate([mlstm_wif[:, :, 0:N_HEAD_C], gate_pad,
                               mlstm_wif[:, :, N_HEAD_C:], gate_pad], axis=-1)
    wts = {
        'vec': _pack_rows(P, [
            (V_NORM1, norm1), (V_CONV_A_B, conv_a_b),
            (V_LRU_B, jnp.concatenate([lru_br, lru_bi], axis=-1)), (V_LAM, lru_lambda),
            (V_NORM_A, norm_a), (V_MU, rwkv_mu), (V_W0, rwkv_w0), (V_A0, rwkv_a0),
            (V_KK, rwkv_kk), (V_KA, rwkv_ka), (V_RK, rwkv_rk.reshape(P, D_B)),
            (V_LNW, rwkv_lnw), (V_LNB, rwkv_lnb), (V_CONV_C_B, conv_c_b),
            (V_BIF, mlstm_bif[:, 0:N_HEAD_C]), (V_GN, mlstm_gn),
            (V_CONV_A_W, conv_a_w), (V_CONV_C_W, conv_c_w), (V_BIF2, mlstm_bif[:, N_HEAD_C:])]),
        'w_in': w_in.astype(BF16),
        'lruw': jnp.concatenate([_block_diag(lru_wr), _block_diag(lru_wi)], axis=-1).astype(BF16),
        'w2': rwkv_w2.astype(BF16),
        'a2': rwkv_a2.astype(BF16),
        'g2': rwkv_g2.astype(BF16),
        'wqk': jnp.concatenate([_block_diag(mlstm_wq), _block_diag(mlstm_wk)], axis=-1).astype(BF16),
        'wif': wif_pad.astype(BF16),
        'w_out': w_out.astype(BF16),
        'ffn_vec': jnp.concatenate([norm2[:, None, :],
                                    jnp.broadcast_to(norm_f[None, None, :], (P, 1, D_MODEL)),
                                    jnp.zeros((P, 6, D_MODEL), F32)], axis=1),
        'w_ffn_in': w_ffn_in.astype(BF16),
        'w_ffn_out': w_ffn_out.astype(BF16),
    }

    def to_kernel_layout(conv_a, lru, shift, wkv, conv_c, mem_c, mem_n, mem_m):
        Bn = lru.shape[1]
        return (conv_a, lru.reshape(P, Bn, 1, D_A), shift, wkv, conv_c,
                mem_c, mem_n, mem_m.reshape(P, Bn, 1, N_HEAD_C))

    def from_kernel_layout(conv_a, lru, shift, wkv, conv_c, mem_c, mem_n, mem_m):
        Bn = lru.shape[1]
        return (conv_a, lru.reshape(P, Bn, D_A), shift, wkv, conv_c,
                mem_c, mem_n, mem_m.reshape(P, Bn, N_HEAD_C))

    Bp, Tp, _ = x_prompt.shape
    Bs, Ts, _ = x_sample.shape
    zero_states = (jnp.zeros((P, Bp, CONV_W - 1, D_A), F32),
                   jnp.zeros((P, Bp, D_A), F32),
                   jnp.zeros((P, Bp, 1, D_B_IN), F32),
                   jnp.zeros((P, Bp, N_HEAD_B, HEAD_DIM, HEAD_DIM), F32),
                   jnp.zeros((P, Bp, CONV_W - 1, D_C), F32),
                   jnp.zeros((P, Bp, N_HEAD_C, HEAD_DIM, HEAD_DIM), F32),
                   jnp.zeros((P, Bp, N_HEAD_C, HEAD_DIM), F32),
                   jnp.zeros((P, Bp, N_HEAD_C), F32))
    Lp = MLSTM_CHUNK if Tp % MLSTM_CHUNK == 0 else Tp
    Ls = MLSTM_CHUNK if Ts % MLSTM_CHUNK == 0 else Ts
    TTp = 256 if Tp % 256 == 0 else Lp
    y_prompt, p_states = _run_stream(x_prompt, to_kernel_layout(*zero_states), wts, TTp, Lp, 512)
    y_sample, s_states = _run_stream(
        x_sample, to_kernel_layout(state_conv_a, state_lru, state_shift_b, state_wkv,
                                   state_conv_c, state_mem_c, state_mem_n, state_mem_m),
        wts, Ls, Ls, Bs * Ts)
    return (y_prompt, y_sample) + from_kernel_layout(*p_states) + from_kernel_layout(*s_states)
```

```python
import functools
import math

import jax
import jax.numpy as jnp
from jax import lax
from jax.experimental import pallas as pl
from jax.experimental.pallas import tpu as pltpu

F32 = jnp.float32
BF16 = jnp.bfloat16

D_MODEL = 1024
DEPTH = 4
HEAD_DIM = 64
D_A = 256
N_BLK_A = 4
CONV_W = 4
LRU_C = 8.0
D_B = 384
N_HEAD_B = 6
LORA_W = 64
LORA_A = 64
LORA_G = 128
D_B_IN = 3 * D_B + LORA_W + LORA_A + LORA_G
D_C = 384
N_HEAD_C = 6
D_IN = 2 * D_A + D_B_IN + 3 * D_C
D_FF = 2816
RMS_EPS = 1e-6
GN_EPS_B = 64e-5
GN_EPS_C = 1e-6
MLSTM_CHUNK = 64

OFF_PB = 2 * D_A
OFF_PC = OFF_PB + D_B_IN
IN_SPLITS = ((0, OFF_PB), (OFF_PB, 2048), (2048, D_IN))
HIST = 8
GATE_W = 128
PAIR_W = 2 * HEAD_DIM
N_PAIR_C = N_HEAD_C // 2

(V_NORM1, V_CONV_A_B, V_LRU_B, V_LAM, V_NORM_A, V_MU, V_W0, V_A0, V_KK, V_KA, V_RK,
 V_LNW, V_LNB, V_CONV_C_B, V_BIF, V_GN, V_CONV_A_W, V_CONV_C_W, V_BIF2) = (
     0, 1, 2, 3, 4, 5, 6, 7, 8, 9, 10, 11, 12, 13, 14, 15, 16, 20, 24)
N_VEC = 32
VEC_W = D_B_IN

VMEM_LIMIT_MIXER = 56 * 1024 * 1024
G_PROMPT = 2
G_SAMPLE = 8
VMEM_LIMIT_FFN = 56 * 1024 * 1024


def _bdot(a, b):
    return jnp.dot(a.astype(BF16), b.astype(BF16), preferred_element_type=F32)


def _bdot_nt(a, b):
    return lax.dot_general(a.astype(BF16), b.astype(BF16), (((1,), (1,)), ((), ())),
                           preferred_element_type=F32)


def _split(a, n):
    parts = []
    rem = a
    for i in range(n):
        p = rem.astype(BF16)
        parts.append(p)
        if i + 1 < n:
            rem = rem - p.astype(F32)
    return parts


def _edot_r(a, e, n=3):
    acc = None
    for p in _split(a, n):
        t = jnp.dot(p, e, preferred_element_type=F32)
        acc = t if acc is None else acc + t
    return acc


def _edot_l(e, a, n=3):
    acc = None
    for p in _split(a, n):
        t = jnp.dot(e, p, preferred_element_type=F32)
        acc = t if acc is None else acc + t
    return acc


def _xdot(a, b, nt=False):
    return _bdot_nt(a, b) if nt else _bdot(a, b)


def _softplus(z):
    return jnp.maximum(z, 0.0) + jnp.log(1.0 + jnp.exp(-jnp.abs(z)))


def _sigmoid(z):
    return 0.5 * jnp.tanh(0.5 * z) + 0.5


def _rms(x, g):
    return x * lax.rsqrt(jnp.mean(x * x, axis=-1, keepdims=True) + RMS_EPS) * g


def _iota2(shape, dim):
    return lax.broadcasted_iota(jnp.int32, shape, dim)


N_STATE = 8
N_WEIGHT = 9
N_SCRATCH = 9


def _mixer_kernel(TT, L, G, *refs):
    n_in = 1 + N_STATE
    ins, wts_ = refs[:n_in], refs[n_in:n_in + N_WEIGHT]
    outs = refs[n_in + N_WEIGHT:2 * n_in + N_WEIGHT]
    scr = refs[2 * n_in + N_WEIGHT:]
    progs = [_stream_program(TT, L,
                             *[r.at[pl.ds(s, 1)] for r in ins], *wts_,
                             *[r.at[pl.ds(s, 1)] for r in outs], *[r.at[s] for r in scr])
             for s in range(G)]
    while progs:
        progs = [p for p in progs if next(p, StopIteration) is not StopIteration]


def _stream_program(TT, L,
                    x_ref, ca_ref, lru_ref, sh_ref, wkv_ref, cc_ref, mc_ref, mn_ref, mm_ref,
                    vec_ref, w_in_ref, lruw_ref, w2_ref, a2_ref, g2_ref, wqk_ref, wif_ref, w_out_ref,
                    xo_ref, ca_o, lru_o, sh_o, wkv_o, cc_o, mc_o, mn_o, mm_o,
                    proj_s, rk_s, ml_s, yb_s, ybc_s, wkv_s, cbd_s, nrow_s, mrow_s):
    j = pl.program_id(1)
    NC = TT // L
    LOG_L = int(math.log2(L))
    CH = range(NC)
    HB = range(N_HEAD_B)
    sls = [slice(h * HEAD_DIM, (h + 1) * HEAD_DIM) for h in HB]
    rws = [slice(c * L, (c + 1) * L) for c in CH]

    def vec(row, width):
        return vec_ref[0, row:row + 1, 0:width]

    @pl.when(j == 0)
    def _init():
        proj_s[0:HIST, :] = jnp.zeros((HIST, D_IN), F32)
        proj_s[HIST - 3:HIST, 0:D_A] = ca_ref[0]
        proj_s[HIST - 1:HIST, OFF_PB:OFF_PB + D_B_IN] = sh_ref[0]
        proj_s[HIST - 3:HIST, OFF_PC:OFF_PC + D_C] = cc_ref[0]
        lru_o[...] = lru_ref[...]
        for h in HB:
            wkv_s[:, sls[h]] = wkv_ref[0, h].T
        cbd_s[...] = jnp.zeros((N_PAIR_C, PAIR_W, PAIR_W), F32)
        mrow_s[...] = jnp.zeros((1, GATE_W), F32)
        for h in range(N_HEAD_C):
            cbd_s[h // 2, sls[h % 2], sls[h % 2]] = mc_ref[0, h]
            nrow_s[:, sls[h]] = mn_ref[0, h:h + 1, :]
        mrow_s[:, 0:N_HEAD_C] = mm_ref[0]

    x = x_ref[0]
    hn = _rms(x, vec(V_NORM1, D_MODEL))
    hn_b = hn.astype(BF16)
    for lo, hi in IN_SPLITS:
        proj_s[HIST:HIST + TT, lo:hi] = jnp.dot(hn_b, w_in_ref[0, :, lo:hi],
                                                preferred_element_type=F32)
    out_parts = {}

    def taps(col0, width, w_row0):
        acc = None
        for t in range(CONV_W):
            r0 = HIST - (CONV_W - 1) + t
            term = vec(w_row0 + t, width) * proj_s[r0:r0 + TT, col0:col0 + width]
            acc = term if acc is None else acc + term
        return acc

    row_t = _iota2((TT, 1), 0)

    seg_r = _iota2((D_B, D_B), 0) >> 6
    seg_c = _iota2((D_B, D_B), 1) >> 6
    ones_bd = jnp.where(seg_r == seg_c, 1.0, 0.0).astype(BF16)
    seg_ones = jnp.where((_iota2((D_C, GATE_W), 0) >> 6) == _iota2((D_C, GATE_W), 1),
                         1.0, 0.0).astype(BF16)
    expand_m = jnp.where(_iota2((GATE_W, D_C), 0) == (_iota2((GATE_W, D_C), 1) >> 6),
                         1.0, 0.0).astype(BF16)
    tt_r = _iota2((TT, TT), 0)
    tt_c = _iota2((TT, TT), 1)
    same_chunk = (tt_r >> LOG_L) == (tt_c >> LOG_L)
    blk_tril = jnp.where(same_chunk & (tt_r >= tt_c), 1.0, 0.0).astype(BF16)

    def segsum(z):
        return _edot_r(z, ones_bd, n=1)

    def expand(z, n=2):
        return _edot_r(z, expand_m, n=n)

    def group_a():
        xa = taps(0, D_A, V_CONV_A_W) + vec(V_CONV_A_B, D_A)
        gates_a = _bdot(xa, lruw_ref[0]) + vec(V_LRU_B, 2 * D_A)
        gate_r = _sigmoid(gates_a[:, 0:D_A])
        gate_i = _sigmoid(gates_a[:, D_A:2 * D_A])
        log_a = (-LRU_C) * gate_r * _softplus(-vec(V_LAM, D_A))
        a_cum = jnp.exp(log_a)
        one_m_a2 = 1.0 - jnp.exp(2.0 * log_a)
        h_loc = jnp.where(one_m_a2 > 0.0, one_m_a2 * lax.rsqrt(one_m_a2), 0.0) * (gate_i * xa)
        yield
        d = 1
        while d < TT:
            keep = row_t >= d
            a_sh = jnp.where(keep, pltpu.roll(a_cum, d, 0), 1.0)
            h_sh = jnp.where(keep, pltpu.roll(h_loc, d, 0), 0.0)
            h_loc = a_cum * h_sh + h_loc
            a_cum = a_cum * a_sh
            d *= 2
            yield
        h_lru = h_loc + a_cum * lru_o[0]
        lru_o[0] = h_lru[TT - 1:TT, :]
        pag = proj_s[HIST:HIST + TT, D_A:2 * D_A]
        gelu = 0.5 * pag * (1.0 + jnp.tanh(math.sqrt(2.0 / math.pi)
                                           * (pag + 0.044715 * (pag * pag * pag))))
        ya = _rms(h_lru, vec(V_NORM_A, D_A)) * gelu
        yield
        out_parts['a'] = jnp.dot(ya.astype(BF16), w_out_ref[0, 0:D_A, :],
                                 preferred_element_type=F32)

    def group_b():
        pb = proj_s[HIST:HIST + TT, OFF_PB:OFF_PB + D_B_IN]
        prev = proj_s[HIST - 1:HIST - 1 + TT, OFF_PB:OFF_PB + D_B_IN]
        xs = pb + (prev - pb) * vec(V_MU, D_B_IN)
        r_b = xs[:, 0:D_B]
        k_b = xs[:, D_B:2 * D_B]
        v_b = xs[:, 2 * D_B:3 * D_B]
        o3 = 3 * D_B
        wd = xs[:, o3:o3 + LORA_W]
        ad = xs[:, o3 + LORA_W:o3 + LORA_W + LORA_A]
        gd = xs[:, o3 + LORA_W + LORA_A:D_B_IN]
        w_log = -_softplus(-(vec(V_W0, D_B) + _bdot(jnp.tanh(wd), w2_ref[0]))) - 0.5
        logw = -jnp.exp(w_log)
        a_b = _sigmoid(vec(V_A0, D_B) + _bdot(ad, a2_ref[0]))
        g_b = _bdot(_sigmoid(gd), g2_ref[0])
        yield
        kk = k_b * vec(V_KK, D_B)
        kk = kk * lax.rsqrt(jnp.maximum(segsum(kk * kk), 1e-24))
        k_b = k_b * (1.0 + (a_b - 1.0) * vec(V_KA, D_B))
        bonus = segsum(r_b * k_b * vec(V_RK, D_B)) * v_b
        ka_b = kk * a_b
        yield
        cl = _edot_l(blk_tril, logw)
        cl_last = [cl[c * L + L - 1:c * L + L, :] for c in CH]
        ctot = jnp.concatenate([jnp.broadcast_to(cl_last[c], (L, D_B)) for c in CH], axis=0)
        w_inv = jnp.exp(-cl)
        to_end = jnp.exp(ctot - cl)
        rk_s[0] = kk * jnp.exp(cl - logw)
        rk_s[1] = k_b * w_inv
        rk_s[2] = ka_b * w_inv
        rk_s[3] = r_b * jnp.exp(cl)
        rk_s[4] = v_b
        rk_s[5] = k_b * to_end
        rk_s[6] = ka_b * to_end
        w_end = [jnp.exp(cl_last[c]) for c in CH]

        yield
        GROUPS = ((0, 4), (4, 2))

        def group_consts(nh):
            nl, w = nh * L, nh * HEAD_DIM
            t_l = _iota2((L, nl), 0)
            s_l = _iota2((L, nl), 1) & (L - 1)
            return dict(
                nat=jnp.where((_iota2((nl, w), 0) >> LOG_L) == (_iota2((nl, w), 1) >> 6),
                              1.0, 0.0).astype(BF16),
                sq=jnp.where((_iota2((nl, nl), 0) >> LOG_L) == (_iota2((nl, nl), 1) >> LOG_L),
                             1.0, 0.0).astype(BF16),
                t=t_l, s=s_l, strict=t_l > s_l, incl=t_l >= s_l,
                eye=jnp.where(t_l == s_l, 1.0, 0.0))

        gconst = {nh: group_consts(nh) for _, nh in GROUPS}

        def bdiag(z, mask):
            zb = z.astype(BF16)
            return jnp.concatenate([zb] * (mask.shape[0] // L), axis=0) * mask

        def mm(a_, b_bf16):
            return jnp.dot(a_.astype(BF16), b_bf16, preferred_element_type=F32)

        def mm_nt(a_, b_bf16):
            return lax.dot_general(a_.astype(BF16), b_bf16, (((1,), (1,)), ((), ())),
                                   preferred_element_type=F32)

        CG = [(c, g) for c in CH for g in range(len(GROUPS))]

        def gslice(i, c, g):
            h0, nh = GROUPS[g]
            return rk_s[i, rws[c], h0 * HEAD_DIM:(h0 + nh) * HEAD_DIM]

        def gc(g):
            return gconst[GROUPS[g][1]]

        qg = {p: gslice(0, *p) for p in CG}
        rg = {p: gslice(3, *p) for p in CG}
        vg = {p: gslice(4, *p) for p in CG}
        qr = {p: jnp.concatenate([qg[p], rg[p]], axis=0) for p in CG}
        sa = {(c, g): mm_nt(qr[c, g], bdiag(gslice(2, c, g), gc(g)['nat'])) for c, g in CG}
        sk = {(c, g): mm_nt(qr[c, g], bdiag(gslice(1, c, g), gc(g)['nat'])) for c, g in CG}
        n_m = {(c, g): jnp.where(gc(g)['strict'], sa[c, g][0:L], 0.0) for c, g in CG}
        mra_m = {(c, g): jnp.where(gc(g)['incl'], sa[c, g][L:2 * L], 0.0) for c, g in CG}
        mk2 = {(c, g): jnp.concatenate([jnp.where(gc(g)['strict'], sk[c, g][0:L], 0.0),
                                        jnp.where(gc(g)['incl'], sk[c, g][L:2 * L], 0.0)], axis=0)
               for c, g in CG}
        yield
        bd_v = {(c, g): bdiag(vg[c, g], gc(g)['nat']) for c, g in CG}
        mv2 = {p: mm(mk2[p], bd_v[p]) for p in CG}
        mkv = {p: mv2[p][0:L] for p in CG}
        mrkv = {p: mv2[p][L:2 * L] for p in CG}
        yield
        xinv = {(c, g): gc(g)['eye'] - jnp.where(((gc(g)['t'] & 1) == 1) & (gc(g)['s'] == gc(g)['t'] - 1),
                                                  n_m[c, g], 0.0) for c, g in CG}
        for lg in range(1, LOG_L):
            lvl = {nh: (((k['t'] >> lg) & 1) == 1) & ((k['s'] >> lg) == (k['t'] >> lg) - 1)
                   for nh, k in gconst.items()}
            xc = {(c, g): mm(xinv[c, g], bdiag(jnp.where(lvl[GROUPS[g][1]], n_m[c, g], 0.0), gc(g)['sq']))
                  for c, g in CG}
            yield
            xinv = {(c, g): xinv[c, g] - mm(xc[c, g], bdiag(xinv[c, g], gc(g)['sq'])) for c, g in CG}
            yield
        p1 = {(c, g): mm(xinv[c, g], bdiag(qg[c, g], gc(g)['nat'])) for c, g in CG}
        p2 = {(c, g): mm(xinv[c, g], bdiag(mkv[c, g], gc(g)['nat'])) for c, g in CG}
        yield
        bd_p1 = {(c, g): bdiag(p1[c, g], gc(g)['nat']) for c, g in CG}
        bd_p2 = {(c, g): bdiag(p2[c, g], gc(g)['nat']) for c, g in CG}
        rq = {p: rg[p] - mm(mra_m[p], bd_p1[p]) for p in CG}
        yc = {p: mrkv[p] - mm(mra_m[p], bd_p2[p]) for p in CG}

        yield
        def head_cat(z, nh):
            return jnp.concatenate([z[i * HEAD_DIM:(i + 1) * HEAD_DIM, :] for i in range(nh)], axis=1)

        def glanes(g):
            h0, nh = GROUPS[g]
            return slice(h0 * HEAD_DIM, (h0 + nh) * HEAD_DIM)

        sq64 = {nh: jnp.where((_iota2((nh * HEAD_DIM, nh * HEAD_DIM), 0) >> 6)
                              == (_iota2((nh * HEAD_DIM, nh * HEAD_DIM), 1) >> 6), 1.0, 0.0).astype(BF16)
                for _, nh in GROUPS}
        at_cat = {(c, g): head_cat(rk_s[6, rws[c], glanes(g)].T, GROUPS[g][1]) for c, g in CG}
        kt_cat = {(c, g): head_cat(rk_s[5, rws[c], glanes(g)].T, GROUPS[g][1]) for c, g in CG}
        wc_cat = {(c, g): head_cat(jnp.broadcast_to(w_end[c][:, glanes(g)],
                                                    (HEAD_DIM, GROUPS[g][1] * HEAD_DIM)).T, GROUPS[g][1])
                  for c, g in CG}
        yield
        g_cat = {p: mm(at_cat[p], bd_p1[p]) for p in CG}
        h_cat = {p: mm(kt_cat[p], bd_v[p]) - mm(at_cat[p], bd_p2[p]) for p in CG}
        yield
        NG = range(len(GROUPS))
        s_cat = [wkv_s[:, glanes(g)] for g in NG]
        for c in CH:
            for g in NG:
                nh = GROUPS[g][1]
                bd_s = jnp.concatenate([s_cat[g].astype(BF16)] * nh, axis=0) * sq64[nh]
                ys = mm(jnp.concatenate([rq[c, g], g_cat[c, g]], axis=0), bd_s)
                yb_s[rws[c], glanes(g)] = ys[0:L] + yc[c, g]
                s_cat[g] = wc_cat[c, g] * s_cat[g] - ys[L:L + HEAD_DIM] + h_cat[c, g]
            yield
        for g in NG:
            h0, nh = GROUPS[g]
            wkv_s[:, glanes(g)] = s_cat[g]
            s_tr = s_cat[g].T
            for i in range(nh):
                wkv_o[0, h0 + i] = s_tr[i * HEAD_DIM:(i + 1) * HEAD_DIM, :]

        y_b = yb_s[...]
        y_cen = y_b - segsum(y_b) * (1.0 / HEAD_DIM)
        yield
        y_var = segsum(y_cen * y_cen) * (1.0 / HEAD_DIM)
        y_n = y_cen * lax.rsqrt(y_var + GN_EPS_B) * vec(V_LNW, D_B) + vec(V_LNB, D_B)
        yb_out = (y_n + bonus) * g_b
        yield
        ybc_s[:, 0:D_B] = yb_out

    def group_c():
        xconv = taps(OFF_PC, D_C, V_CONV_C_W) + vec(V_CONV_C_B, D_C)
        xact = xconv * _sigmoid(xconv)
        qk = _bdot(xact, wqk_ref[0])
        q_c = qk[:, 0:D_C]
        k_c = qk[:, D_C:2 * D_C]
        vc = proj_s[HIST:HIST + TT, OFF_PC + D_C:OFF_PC + 2 * D_C]
        gates_c = (_bdot(q_c, wif_ref[0, 0:D_C, :]) + _bdot(k_c, wif_ref[0, D_C:2 * D_C, :])
                   + _bdot(vc, wif_ref[0, 2 * D_C:3 * D_C, :]))
        i_pre = gates_c[:, 0:GATE_W] + vec(V_BIF, GATE_W)
        logf = -_softplus(-(gates_c[:, GATE_W:2 * GATE_W] + vec(V_BIF2, GATE_W)))
        ml_s[0] = q_c
        ml_s[1] = k_c * (HEAD_DIM ** -0.5)

        yield
        b_col = _edot_l(blk_tril, logf)
        yield
        g_col = i_pre - b_col
        g_max = g_col
        pos = row_t & (L - 1)
        d = 1
        while d < L:
            g_max = jnp.where(pos >= d, jnp.maximum(g_max, pltpu.roll(g_max, d, 0)), g_max)
            d *= 2
        yield
        m_loc = b_col + g_max
        g_last = [g_max[c * L + L - 1:c * L + L, :] for c in CH]
        b_end = [b_col[c * L + L - 1:c * L + L, :] for c in CH]
        m_loc_end = [b_end[c] + g_last[c] for c in CH]
        w_end_loc = jnp.concatenate([jnp.exp(g_col[rws[c], :] - g_last[c]) for c in CH], axis=0)
        kw = ml_s[1] * expand(w_end_loc, n=1)
        yield
        s_of_lane = _iota2((L, N_HEAD_C * L), 1) & (L - 1)
        t_of_row = _iota2((L, N_HEAD_C * L), 0)
        causal_t = t_of_row >= s_of_lane
        eye_t = t_of_row == s_of_lane
        bd_l = (_iota2((N_HEAD_C * L, D_C), 0) >> LOG_L) == (_iota2((N_HEAD_C * L, D_C), 1) >> 6)
        seg_ones_l = jnp.where((_iota2((N_HEAD_C * L, GATE_W), 0) >> LOG_L)
                               == _iota2((N_HEAD_C * L, GATE_W), 1), 1.0, 0.0).astype(BF16)
        expand_l = jnp.where(_iota2((GATE_W, N_HEAD_C * L), 0)
                             == (_iota2((GATE_W, N_HEAD_C * L), 1) >> LOG_L), 1.0, 0.0).astype(BF16)
        g_exp_l = _edot_r(g_col, expand_l, n=2)
        g_max_exp_l = _edot_r(g_max, expand_l, n=2)

        pws = [slice(p * PAIR_W, (p + 1) * PAIR_W) for p in range(N_PAIR_C)]
        pair_mask = (_iota2((PAIR_W, PAIR_W), 0) >> 6) == (_iota2((PAIR_W, PAIR_W), 1) >> 6)
        yield
        p_loc, den_loc, c_loc, n_loc = [], [], [], []
        for c in CH:
            q_cc = ml_s[0, rws[c], :]
            k_cc = ml_s[1, rws[c], :]
            v_cc = proj_s[HIST + c * L:HIST + (c + 1) * L, OFF_PC + D_C:OFF_PC + 2 * D_C]
            k_bd = jnp.where(bd_l, jnp.concatenate([k_cc] * N_HEAD_C, axis=0), 0.0)
            v_bd = jnp.where(bd_l, jnp.concatenate([v_cc] * N_HEAD_C, axis=0), 0.0)
            g_flat = jnp.sum(jnp.where(eye_t, g_exp_l[rws[c], :], 0.0), axis=0, keepdims=True)
            dmat = jnp.where(causal_t, jnp.exp(jnp.minimum(g_flat - g_max_exp_l[rws[c], :], 0.0)), 0.0)
            scores = _bdot_nt(q_cc, k_bd) * dmat
            p_loc.append(_bdot(scores, v_bd))
            den_loc.append(_edot_r(scores, seg_ones_l, n=1))
            kw_c = kw[rws[c], :]
            kw_t = kw_c.T
            c_loc.append([jnp.where(pair_mask, _bdot(kw_t[pws[p], :], v_cc[:, pws[p]]), 0.0)
                          for p in range(N_PAIR_C)])
            n_loc.append(jnp.sum(kw_c, axis=0, keepdims=True))
            yield

        m_in = []
        m_cur = mrow_s[...]
        for c in CH:
            m_in.append(m_cur)
            m_cur = jnp.maximum(b_end[c] + m_cur, m_loc_end[c])
        m_out = m_in[1:] + [m_cur]
        mrow_s[...] = m_cur
        scale_rows = jnp.concatenate(
            [jnp.exp(b_end[c] + m_in[c] - m_out[c]) for c in CH]
            + [jnp.exp(m_loc_end[c] - m_out[c]) for c in CH], axis=0)
        scale_exp = expand(scale_rows)
        yield
        c_in, n_in = [], []
        c_cur = [cbd_s[p] for p in range(N_PAIR_C)]
        n_cur = nrow_s[...]
        for c in CH:
            c_in.append(c_cur)
            n_in.append(n_cur)
            w0 = scale_exp[c:c + 1, :]
            wl = scale_exp[NC + c:NC + c + 1, :]
            c_cur = [c_cur[p] * w0[:, pws[p]] + c_loc[c][p] * wl[:, pws[p]] for p in range(N_PAIR_C)]
            n_cur = n_cur * w0 + n_loc[c] * wl
        for p in range(N_PAIR_C):
            cbd_s[p] = c_cur[p]
        nrow_s[...] = n_cur
        for h in range(N_HEAD_C):
            mc_o[0, h] = c_cur[h // 2][sls[h % 2], sls[h % 2]]
            mn_o[0, h:h + 1, :] = n_cur[:, sls[h]]
        mm_o[0] = m_cur[:, 0:N_HEAD_C]

        yield
        m0_b = jnp.concatenate([jnp.broadcast_to(m_in[c], (L, GATE_W)) for c in CH], axis=0)
        log_inter = b_col + m0_b
        m_t = jnp.maximum(log_inter, m_loc)
        w_inter = jnp.exp(log_inter - m_t)
        w_local = jnp.exp(m_loc - m_t)
        q_all = ml_s[0]
        qc0 = jnp.concatenate(
            [jnp.concatenate([_bdot(q_all[rws[c], pws[p]], c_in[c][p]) for p in range(N_PAIR_C)], axis=1)
             for c in CH], axis=0)
        qn0 = jnp.concatenate([_edot_r(q_all[rws[c], :] * n_in[c], seg_ones, n=1) for c in CH], axis=0)
        yield
        den = w_inter * qn0 + w_local * jnp.concatenate(den_loc, axis=0)
        inv_den = 1.0 / jnp.maximum(jnp.abs(den), jnp.exp(-m_t))
        h_c = (expand(w_inter * inv_den) * qc0
               + expand(w_local * inv_den) * jnp.concatenate(p_loc, axis=0))

        h_cen = h_c - segsum(h_c) * (1.0 / HEAD_DIM)
        yield
        h_var = segsum(h_cen * h_cen) * (1.0 / HEAD_DIM)
        zc = proj_s[HIST:HIST + TT, OFF_PC + 2 * D_C:OFF_PC + 3 * D_C]
        yc_out = _sigmoid(zc) * (h_cen * lax.rsqrt(h_var + GN_EPS_C) * vec(V_GN, D_C))
        yield
        ybc_s[:, D_B:D_B + D_C] = yc_out

    pending = [group_a(), group_b(), group_c()]
    while pending:
        pending = [g for g in pending if next(g, StopIteration) is not StopIteration]
        yield

    last3 = proj_s[HIST + TT - 3:HIST + TT, :]
    proj_s[HIST - 3:HIST, :] = last3
    ca_o[0] = last3[:, 0:D_A]
    sh_o[0] = last3[2:3, OFF_PB:OFF_PB + D_B_IN]
    cc_o[0] = last3[:, OFF_PC:OFF_PC + D_C]

    xo_ref[0] = (x_ref[0] + out_parts['a']) + jnp.dot(ybc_s[...].astype(BF16), w_out_ref[0, D_A:D_MODEL, :],
                                                      preferred_element_type=F32)


def _mixer_call(layer, x, states, wts, TT, L, G):
    B, T, _ = x.shape
    NT = T // TT
    assert B % G == 0

    def state_in_spec(shape):
        nd = len(shape)
        return pl.BlockSpec((None, G) + tuple(shape[2:]), lambda b, j: (layer, b) + (0,) * (nd - 2))

    def state_out_spec(shape):
        nd = len(shape)
        return pl.BlockSpec((G,) + tuple(shape[1:]), lambda b, j: (b,) + (0,) * (nd - 1))

    def layer_spec(arr):
        nd = arr.ndim
        return pl.BlockSpec((1,) + tuple(arr.shape[1:]), lambda b, j: (layer,) + (0,) * (nd - 1))

    x_spec = pl.BlockSpec((G, TT, D_MODEL), lambda b, j: (b, j, 0))
    state_in_specs = [state_in_spec(s.shape) for s in states]
    state_out_specs = [state_out_spec(s.shape[1:]) for s in states]
    w_names = ('vec', 'w_in', 'lruw', 'w2', 'a2', 'g2', 'wqk', 'wif', 'w_out')
    w_arrs = [wts[n] for n in w_names]

    out_shape = ([jax.ShapeDtypeStruct(x.shape, F32)]
                 + [jax.ShapeDtypeStruct(s.shape[1:], F32) for s in states])
    scratch = [
        pltpu.VMEM((G, HIST + TT, D_IN), F32),
        pltpu.VMEM((G, 7, TT, D_B), F32),
        pltpu.VMEM((G, 2, TT, D_C), F32),
        pltpu.VMEM((G, TT, D_B), F32),
        pltpu.VMEM((G, TT, D_B + D_C), F32),
        pltpu.VMEM((G, HEAD_DIM, D_B), F32),
        pltpu.VMEM((G, N_PAIR_C, PAIR_W, PAIR_W), F32),
        pltpu.VMEM((G, 1, D_C), F32),
        pltpu.VMEM((G, 1, GATE_W), F32),
    ]
    outs = pl.pallas_call(
        functools.partial(_mixer_kernel, TT, L, G),
        grid=(B // G, NT),
        in_specs=[x_spec] + state_in_specs + [layer_spec(a) for a in w_arrs],
        out_specs=[x_spec] + state_out_specs,
        out_shape=out_shape,
        scratch_shapes=scratch,
        compiler_params=pltpu.CompilerParams(
            dimension_semantics=("arbitrary", "arbitrary"),
            vmem_limit_bytes=VMEM_LIMIT_MIXER),
        name="mixer",
    )(x, *states, *w_arrs)
    return outs[0], tuple(outs[1:])


FF_CHUNK = 768


def _ffn_kernel(final, x_ref, vec_ref, wi_ref, wo_ref, o_ref):
    x = x_ref[...]
    hn = _rms(x, vec_ref[0, 0:1, :]).astype(BF16)
    acc = x
    for lo in range(0, D_FF, FF_CHUNK):
        hi = min(lo + FF_CHUNK, D_FF)
        gate = jnp.dot(hn, wi_ref[0, :, lo:hi], preferred_element_type=F32)
        up = jnp.dot(hn, wi_ref[0, :, D_FF + lo:D_FF + hi], preferred_element_type=F32)
        act = (gate * _sigmoid(gate) * up).astype(BF16)
        acc = acc + jnp.dot(act, wo_ref[0, lo:hi, :], preferred_element_type=F32)
    if final:
        acc = _rms(acc, vec_ref[0, 1:2, :])
    o_ref[...] = acc


def _ffn_call(layer, x2d, wts, TM, final):
    M = x2d.shape[0]
    return pl.pallas_call(
        functools.partial(_ffn_kernel, final),
        grid=(M // TM,),
        in_specs=[
            pl.BlockSpec((TM, D_MODEL), lambda i: (i, 0)),
            pl.BlockSpec((1, 8, D_MODEL), lambda i: (layer, 0, 0)),
            pl.BlockSpec((1, D_MODEL, 2 * D_FF), lambda i: (layer, 0, 0),
                         pipeline_mode=pl.Buffered(1)),
            pl.BlockSpec((1, D_FF, D_MODEL), lambda i: (layer, 0, 0),
                         pipeline_mode=pl.Buffered(1)),
        ],
        out_specs=pl.BlockSpec((TM, D_MODEL), lambda i: (i, 0)),
        out_shape=jax.ShapeDtypeStruct(x2d.shape, F32),
        compiler_params=pltpu.CompilerParams(
            dimension_semantics=("arbitrary",),
            vmem_limit_bytes=VMEM_LIMIT_FFN),
        name="ffn",
    )(x2d, wts['ffn_vec'], wts['w_ffn_in'], wts['w_ffn_out'])


def _block_diag(w):
    P, n, d, e = w.shape
    eye = jnp.eye(n, dtype=w.dtype)
    return (w[:, :, :, None, :] * eye[None, :, None, :, None]).reshape(P, n * d, n * e)


def _pack_rows(P, rows):
    parts = []
    for r, arr in rows:
        arr = arr.reshape(P, -1, arr.shape[-1]).astype(F32)
        assert r == sum(p.shape[1] for p in parts)
        parts.append(jnp.pad(arr, ((0, 0), (0, 0), (0, VEC_W - arr.shape[2]))))
    used = sum(p.shape[1] for p in parts)
    parts.append(jnp.zeros((P, N_VEC - used, VEC_W), F32))
    return jnp.concatenate(parts, axis=1)


def _run_stream(x, states, wts, TT, L, TM, G):
    B, T, _ = x.shape
    new_states = []
    for l in range(DEPTH):
        x, st_new = _mixer_call(l, x, states, wts, TT, L, G)
        new_states.append(st_new)
        x = _ffn_call(l, x.reshape(B * T, D_MODEL), wts, TM, l == DEPTH - 1).reshape(B, T, D_MODEL)
    stacked = tuple(jnp.stack([st[i] for st in new_states]) for i in range(len(states)))
    return x, stacked


def kernel(x_prompt, x_sample, state_conv_a, state_lru, state_shift_b, state_wkv, state_conv_c, state_mem_c, state_mem_n, state_mem_m, norm1, w_in, conv_a_w, conv_a_b, lru_wr, lru_br, lru_wi, lru_bi, lru_lambda, norm_a, rwkv_mu, rwkv_w0, rwkv_w2, rwkv_a0, rwkv_a2, rwkv_g2, rwkv_kk, rwkv_ka, rwkv_rk, rwkv_lnw, rwkv_lnb, conv_c_w, conv_c_b, mlstm_wq, mlstm_wk, mlstm_wif, mlstm_bif, mlstm_gn, w_out, norm2, w_ffn_in, w_ffn_out, norm_f):
    P = DEPTH
    gate_pad = jnp.zeros((P, 3 * D_C, GATE_W - N_HEAD_C), F32)
    wif_pad = jnp.concatenate([mlstm_wif[:, :, 0:N_HEAD_C], gate_pad,
                               mlstm_wif[:, :, N_HEAD_C:], gate_pad], axis=-1)
    wts = {
        'vec': _pack_rows(P, [
            (V_NORM1, norm1), (V_CONV_A_B, conv_a_b),
            (V_LRU_B, jnp.concatenate([lru_br, lru_bi], axis=-1)), (V_LAM, lru_lambda),
            (V_NORM_A, norm_a), (V_MU, rwkv_mu), (V_W0, rwkv_w0), (V_A0, rwkv_a0),
            (V_KK, rwkv_kk), (V_KA, rwkv_ka), (V_RK, rwkv_rk.reshape(P, D_B)),
            (V_LNW, rwkv_lnw), (V_LNB, rwkv_lnb), (V_CONV_C_B, conv_c_b),
            (V_BIF, mlstm_bif[:, 0:N_HEAD_C]), (V_GN, mlstm_gn),
            (V_CONV_A_W, conv_a_w), (V_CONV_C_W, conv_c_w), (V_BIF2, mlstm_bif[:, N_HEAD_C:])]),
        'w_in': w_in.astype(BF16),
        'lruw': jnp.concatenate([_block_diag(lru_wr), _block_diag(lru_wi)], axis=-1).astype(BF16),
        'w2': rwkv_w2.astype(BF16),
        'a2': rwkv_a2.astype(BF16),
        'g2': rwkv_g2.astype(BF16),
        'wqk': jnp.concatenate([_block_diag(mlstm_wq), _block_diag(mlstm_wk)], axis=-1).astype(BF16),
        'wif': wif_pad.astype(BF16),
        'w_out': w_out.astype(BF16),
        'ffn_vec': jnp.concatenate([norm2[:, None, :],
                                    jnp.broadcast_to(norm_f[None, None, :], (P, 1, D_MODEL)),
                                    jnp.zeros((P, 6, D_MODEL), F32)], axis=1),
        'w_ffn_in': w_ffn_in.astype(BF16),
        'w_ffn_out': w_ffn_out.astype(BF16),
    }

    def to_kernel_layout(conv_a, lru, shift, wkv, conv_c, mem_c, mem_n, mem_m):
        Bn = lru.shape[1]
        return (conv_a, lru.reshape(P, Bn, 1, D_A), shift, wkv, conv_c,
                mem_c, mem_n, mem_m.reshape(P, Bn, 1, N_HEAD_C))

    def from_kernel_layout(conv_a, lru, shift, wkv, conv_c, mem_c, mem_n, mem_m):
        Bn = lru.shape[1]
        return (conv_a, lru.reshape(P, Bn, D_A), shift, wkv, conv_c,
                mem_c, mem_n, mem_m.reshape(P, Bn, N_HEAD_C))

    Bp, Tp, _ = x_prompt.shape
    Bs, Ts, _ = x_sample.shape
    zero_states = (jnp.zeros((P, Bp, CONV_W - 1, D_A), F32),
                   jnp.zeros((P, Bp, D_A), F32),
                   jnp.zeros((P, Bp, 1, D_B_IN), F32),
                   jnp.zeros((P, Bp, N_HEAD_B, HEAD_DIM, HEAD_DIM), F32),
                   jnp.zeros((P, Bp, CONV_W - 1, D_C), F32),
                   jnp.zeros((P, Bp, N_HEAD_C, HEAD_DIM, HEAD_DIM), F32),
                   jnp.zeros((P, Bp, N_HEAD_C, HEAD_DIM), F32),
                   jnp.zeros((P, Bp, N_HEAD_C), F32))
    Lp = MLSTM_CHUNK if Tp % MLSTM_CHUNK == 0 else Tp
    Ls = MLSTM_CHUNK if Ts % MLSTM_CHUNK == 0 else Ts
    TTp = 256 if Tp % 256 == 0 else Lp
    y_prompt, p_states = _run_stream(x_prompt, to_kernel_layout(*zero_states), wts, TTp, Lp, 512, G_PROMPT)
    y_sample, s_states = _run_stream(
        x_sample, to_kernel_layout(state_conv_a, state_lru, state_shift_b, state_wkv,
                                   state_conv_c, state_mem_c, state_mem_n, state_mem_m),
        wts, Ls, Ls, Bs * Ts, G_SAMPLE)
    return (y_prompt, y_sample) + from_kernel_layout(*p_states) + from_kernel_layout(*s_states)
```

```python
import functools
import math

import jax
import jax.numpy as jnp
from jax import lax
from jax.experimental import pallas as pl
from jax.experimental.pallas import tpu as pltpu

F32 = jnp.float32
BF16 = jnp.bfloat16

D_MODEL = 1024
DEPTH = 4
HEAD_DIM = 64
D_A = 256
N_BLK_A = 4
CONV_W = 4
LRU_C = 8.0
D_B = 384
N_HEAD_B = 6
LORA_W = 64
LORA_A = 64
LORA_G = 128
D_B_IN = 3 * D_B + LORA_W + LORA_A + LORA_G
D_C = 384
N_HEAD_C = 6
D_IN = 2 * D_A + D_B_IN + 3 * D_C
D_FF = 2816
RMS_EPS = 1e-6
GN_EPS_B = 64e-5
GN_EPS_C = 1e-6
MLSTM_CHUNK = 64

OFF_PB = 2 * D_A
OFF_PC = OFF_PB + D_B_IN
IN_SPLITS = ((0, OFF_PB), (OFF_PB, 2048), (2048, D_IN))
HIST = 8
GATE_W = 128
PAIR_W = 2 * HEAD_DIM
N_PAIR_C = N_HEAD_C // 2

(V_NORM1, V_CONV_A_B, V_LRU_B, V_LAM, V_NORM_A, V_MU, V_W0, V_A0, V_KK, V_KA, V_RK,
 V_LNW, V_LNB, V_CONV_C_B, V_BIF, V_GN, V_CONV_A_W, V_CONV_C_W, V_BIF2) = (
     0, 1, 2, 3, 4, 5, 6, 7, 8, 9, 10, 11, 12, 13, 14, 15, 16, 20, 24)
N_VEC = 32
VEC_W = D_B_IN

VMEM_LIMIT_MIXER = 56 * 1024 * 1024
G_PROMPT = 2
G_SAMPLE = 8
VMEM_LIMIT_FFN = 56 * 1024 * 1024


def _bdot(a, b):
    return jnp.dot(a.astype(BF16), b.astype(BF16), preferred_element_type=F32)


def _bdot_nt(a, b):
    return lax.dot_general(a.astype(BF16), b.astype(BF16), (((1,), (1,)), ((), ())),
                           preferred_element_type=F32)


def _split(a, n):
    parts = []
    rem = a
    for i in range(n):
        p = rem.astype(BF16)
        parts.append(p)
        if i + 1 < n:
            rem = rem - p.astype(F32)
    return parts


def _edot_r(a, e, n=3):
    acc = None
    for p in _split(a, n):
        t = jnp.dot(p, e, preferred_element_type=F32)
        acc = t if acc is None else acc + t
    return acc


def _edot_l(e, a, n=3):
    acc = None
    for p in _split(a, n):
        t = jnp.dot(e, p, preferred_element_type=F32)
        acc = t if acc is None else acc + t
    return acc


def _xdot(a, b, nt=False):
    return _bdot_nt(a, b) if nt else _bdot(a, b)


def _softplus(z):
    return jnp.maximum(z, 0.0) + jnp.log(1.0 + jnp.exp(-jnp.abs(z)))


def _sigmoid(z):
    return 0.5 * jnp.tanh(0.5 * z) + 0.5


def _rms(x, g):
    return x * lax.rsqrt(jnp.mean(x * x, axis=-1, keepdims=True) + RMS_EPS) * g


def _iota2(shape, dim):
    return lax.broadcasted_iota(jnp.int32, shape, dim)


N_STATE = 8
N_WEIGHT = 9
N_SCRATCH = 9


def _mixer_kernel(TT, L, G, ffn_final, *refs):
    n_in = 1 + N_STATE
    n_w = N_WEIGHT + (0 if ffn_final is None else 3)
    ins, wts_ = refs[:n_in], refs[n_in:n_in + N_WEIGHT]
    ffn_w = refs[n_in + N_WEIGHT:n_in + n_w]
    outs = refs[n_in + n_w:2 * n_in + n_w]
    scr = refs[2 * n_in + n_w:]
    progs = [_stream_program(TT, L,
                             *[r.at[pl.ds(s, 1)] for r in ins], *wts_,
                             *[r.at[pl.ds(s, 1)] for r in outs], *[r.at[s] for r in scr])
             for s in range(G)]
    while progs:
        progs = [p for p in progs if next(p, StopIteration) is not StopIteration]
    if ffn_final is not None:
        xo = outs[0]
        xo[...] = _ffn_rows(ffn_final, xo[...].reshape(G * TT, D_MODEL), *ffn_w).reshape(xo.shape)


def _stream_program(TT, L,
                    x_ref, ca_ref, lru_ref, sh_ref, wkv_ref, cc_ref, mc_ref, mn_ref, mm_ref,
                    vec_ref, w_in_ref, lruw_ref, w2_ref, a2_ref, g2_ref, wqk_ref, wif_ref, w_out_ref,
                    xo_ref, ca_o, lru_o, sh_o, wkv_o, cc_o, mc_o, mn_o, mm_o,
                    proj_s, rk_s, ml_s, yb_s, ybc_s, wkv_s, cbd_s, nrow_s, mrow_s):
    j = pl.program_id(1)
    NC = TT // L
    LOG_L = int(math.log2(L))
    CH = range(NC)
    HB = range(N_HEAD_B)
    sls = [slice(h * HEAD_DIM, (h + 1) * HEAD_DIM) for h in HB]
    rws = [slice(c * L, (c + 1) * L) for c in CH]

    def vec(row, width):
        return vec_ref[0, row:row + 1, 0:width]

    @pl.when(j == 0)
    def _init():
        proj_s[0:HIST, :] = jnp.zeros((HIST, D_IN), F32)
        proj_s[HIST - 3:HIST, 0:D_A] = ca_ref[0]
        proj_s[HIST - 1:HIST, OFF_PB:OFF_PB + D_B_IN] = sh_ref[0]
        proj_s[HIST - 3:HIST, OFF_PC:OFF_PC + D_C] = cc_ref[0]
        lru_o[...] = lru_ref[...]
        for h in HB:
            wkv_s[:, sls[h]] = wkv_ref[0, h].T
        cbd_s[...] = jnp.zeros((N_PAIR_C, PAIR_W, PAIR_W), F32)
        mrow_s[...] = jnp.zeros((1, GATE_W), F32)
        for h in range(N_HEAD_C):
            cbd_s[h // 2, sls[h % 2], sls[h % 2]] = mc_ref[0, h]
            nrow_s[:, sls[h]] = mn_ref[0, h:h + 1, :]
        mrow_s[:, 0:N_HEAD_C] = mm_ref[0]

    x = x_ref[0]
    hn = _rms(x, vec(V_NORM1, D_MODEL))
    hn_b = hn.astype(BF16)
    for lo, hi in IN_SPLITS:
        proj_s[HIST:HIST + TT, lo:hi] = jnp.dot(hn_b, w_in_ref[0, :, lo:hi],
                                                preferred_element_type=F32)
    out_parts = {}

    def taps(col0, width, w_row0):
        acc = None
        for t in range(CONV_W):
            r0 = HIST - (CONV_W - 1) + t
            term = vec(w_row0 + t, width) * proj_s[r0:r0 + TT, col0:col0 + width]
            acc = term if acc is None else acc + term
        return acc

    row_t = _iota2((TT, 1), 0)

    seg_r = _iota2((D_B, D_B), 0) >> 6
    seg_c = _iota2((D_B, D_B), 1) >> 6
    ones_bd = jnp.where(seg_r == seg_c, 1.0, 0.0).astype(BF16)
    seg_ones = jnp.where((_iota2((D_C, GATE_W), 0) >> 6) == _iota2((D_C, GATE_W), 1),
                         1.0, 0.0).astype(BF16)
    expand_m = jnp.where(_iota2((GATE_W, D_C), 0) == (_iota2((GATE_W, D_C), 1) >> 6),
                         1.0, 0.0).astype(BF16)
    tt_r = _iota2((TT, TT), 0)
    tt_c = _iota2((TT, TT), 1)
    same_chunk = (tt_r >> LOG_L) == (tt_c >> LOG_L)
    blk_tril = jnp.where(same_chunk & (tt_r >= tt_c), 1.0, 0.0).astype(BF16)

    def segsum(z):
        return _edot_r(z, ones_bd, n=1)

    def expand(z, n=2):
        return _edot_r(z, expand_m, n=n)

    def group_a():
        xa = taps(0, D_A, V_CONV_A_W) + vec(V_CONV_A_B, D_A)
        gates_a = _bdot(xa, lruw_ref[0]) + vec(V_LRU_B, 2 * D_A)
        gate_r = _sigmoid(gates_a[:, 0:D_A])
        gate_i = _sigmoid(gates_a[:, D_A:2 * D_A])
        log_a = (-LRU_C) * gate_r * _softplus(-vec(V_LAM, D_A))
        a_cum = jnp.exp(log_a)
        one_m_a2 = 1.0 - jnp.exp(2.0 * log_a)
        h_loc = jnp.where(one_m_a2 > 0.0, one_m_a2 * lax.rsqrt(one_m_a2), 0.0) * (gate_i * xa)
        yield
        d = 1
        while d < TT:
            keep = row_t >= d
            a_sh = jnp.where(keep, pltpu.roll(a_cum, d, 0), 1.0)
            h_sh = jnp.where(keep, pltpu.roll(h_loc, d, 0), 0.0)
            h_loc = a_cum * h_sh + h_loc
            a_cum = a_cum * a_sh
            d *= 2
            yield
        h_lru = h_loc + a_cum * lru_o[0]
        lru_o[0] = h_lru[TT - 1:TT, :]
        pag = proj_s[HIST:HIST + TT, D_A:2 * D_A]
        gelu = 0.5 * pag * (1.0 + jnp.tanh(math.sqrt(2.0 / math.pi)
                                           * (pag + 0.044715 * (pag * pag * pag))))
        ya = _rms(h_lru, vec(V_NORM_A, D_A)) * gelu
        yield
        out_parts['a'] = jnp.dot(ya.astype(BF16), w_out_ref[0, 0:D_A, :],
                                 preferred_element_type=F32)

    def group_b():
        pb = proj_s[HIST:HIST + TT, OFF_PB:OFF_PB + D_B_IN]
        prev = proj_s[HIST - 1:HIST - 1 + TT, OFF_PB:OFF_PB + D_B_IN]
        xs = pb + (prev - pb) * vec(V_MU, D_B_IN)
        r_b = xs[:, 0:D_B]
        k_b = xs[:, D_B:2 * D_B]
        v_b = xs[:, 2 * D_B:3 * D_B]
        o3 = 3 * D_B
        wd = xs[:, o3:o3 + LORA_W]
        ad = xs[:, o3 + LORA_W:o3 + LORA_W + LORA_A]
        gd = xs[:, o3 + LORA_W + LORA_A:D_B_IN]
        w_log = -_softplus(-(vec(V_W0, D_B) + _bdot(jnp.tanh(wd), w2_ref[0]))) - 0.5
        logw = -jnp.exp(w_log)
        a_b = _sigmoid(vec(V_A0, D_B) + _bdot(ad, a2_ref[0]))
        g_b = _bdot(_sigmoid(gd), g2_ref[0])
        yield
        kk = k_b * vec(V_KK, D_B)
        kk = kk * lax.rsqrt(jnp.maximum(segsum(kk * kk), 1e-24))
        k_b = k_b * (1.0 + (a_b - 1.0) * vec(V_KA, D_B))
        bonus = segsum(r_b * k_b * vec(V_RK, D_B)) * v_b
        ka_b = kk * a_b
        yield
        cl = _edot_l(blk_tril, logw)
        cl_last = [cl[c * L + L - 1:c * L + L, :] for c in CH]
        ctot = jnp.concatenate([jnp.broadcast_to(cl_last[c], (L, D_B)) for c in CH], axis=0)
        w_inv = jnp.exp(-cl)
        to_end = jnp.exp(ctot - cl)
        rk_s[0] = kk * jnp.exp(cl - logw)
        rk_s[1] = k_b * w_inv
        rk_s[2] = ka_b * w_inv
        rk_s[3] = r_b * jnp.exp(cl)
        rk_s[4] = v_b
        rk_s[5] = k_b * to_end
        rk_s[6] = ka_b * to_end
        w_end = [jnp.exp(cl_last[c]) for c in CH]

        yield
        GROUPS = ((0, 4), (4, 2))

        def group_consts(nh):
            nl, w = nh * L, nh * HEAD_DIM
            t_l = _iota2((L, nl), 0)
            s_l = _iota2((L, nl), 1) & (L - 1)
            return dict(
                nat=jnp.where((_iota2((nl, w), 0) >> LOG_L) == (_iota2((nl, w), 1) >> 6),
                              1.0, 0.0).astype(BF16),
                sq=jnp.where((_iota2((nl, nl), 0) >> LOG_L) == (_iota2((nl, nl), 1) >> LOG_L),
                             1.0, 0.0).astype(BF16),
                t=t_l, s=s_l, strict=t_l > s_l, incl=t_l >= s_l,
                eye=jnp.where(t_l == s_l, 1.0, 0.0))

        gconst = {nh: group_consts(nh) for _, nh in GROUPS}

        def bdiag(z, mask):
            zb = z.astype(BF16)
            return jnp.concatenate([zb] * (mask.shape[0] // L), axis=0) * mask

        def mm(a_, b_bf16):
            return jnp.dot(a_.astype(BF16), b_bf16, preferred_element_type=F32)

        def mm_nt(a_, b_bf16):
            return lax.dot_general(a_.astype(BF16), b_bf16, (((1,), (1,)), ((), ())),
                                   preferred_element_type=F32)

        CG = [(c, g) for c in CH for g in range(len(GROUPS))]

        def gslice(i, c, g):
            h0, nh = GROUPS[g]
            return rk_s[i, rws[c], h0 * HEAD_DIM:(h0 + nh) * HEAD_DIM]

        def gc(g):
            return gconst[GROUPS[g][1]]

        qg = {p: gslice(0, *p) for p in CG}
        rg = {p: gslice(3, *p) for p in CG}
        vg = {p: gslice(4, *p) for p in CG}
        qr = {p: jnp.concatenate([qg[p], rg[p]], axis=0) for p in CG}
        sa = {(c, g): mm_nt(qr[c, g], bdiag(gslice(2, c, g), gc(g)['nat'])) for c, g in CG}
        sk = {(c, g): mm_nt(qr[c, g], bdiag(gslice(1, c, g), gc(g)['nat'])) for c, g in CG}
        n_m = {(c, g): jnp.where(gc(g)['strict'], sa[c, g][0:L], 0.0) for c, g in CG}
        mra_m = {(c, g): jnp.where(gc(g)['incl'], sa[c, g][L:2 * L], 0.0) for c, g in CG}
        mk2 = {(c, g): jnp.concatenate([jnp.where(gc(g)['strict'], sk[c, g][0:L], 0.0),
                                        jnp.where(gc(g)['incl'], sk[c, g][L:2 * L], 0.0)], axis=0)
               for c, g in CG}
        yield
        bd_v = {(c, g): bdiag(vg[c, g], gc(g)['nat']) for c, g in CG}
        mv2 = {p: mm(mk2[p], bd_v[p]) for p in CG}
        mkv = {p: mv2[p][0:L] for p in CG}
        mrkv = {p: mv2[p][L:2 * L] for p in CG}
        yield
        xinv = {(c, g): gc(g)['eye'] - jnp.where(((gc(g)['t'] & 1) == 1) & (gc(g)['s'] == gc(g)['t'] - 1),
                                                  n_m[c, g], 0.0) for c, g in CG}
        for lg in range(1, LOG_L):
            lvl = {nh: (((k['t'] >> lg) & 1) == 1) & ((k['s'] >> lg) == (k['t'] >> lg) - 1)
                   for nh, k in gconst.items()}
            xc = {(c, g): mm(xinv[c, g], bdiag(jnp.where(lvl[GROUPS[g][1]], n_m[c, g], 0.0), gc(g)['sq']))
                  for c, g in CG}
            yield
            xinv = {(c, g): xinv[c, g] - mm(xc[c, g], bdiag(xinv[c, g], gc(g)['sq'])) for c, g in CG}
            yield
        p1 = {(c, g): mm(xinv[c, g], bdiag(qg[c, g], gc(g)['nat'])) for c, g in CG}
        p2 = {(c, g): mm(xinv[c, g], bdiag(mkv[c, g], gc(g)['nat'])) for c, g in CG}
        yield
        bd_p1 = {(c, g): bdiag(p1[c, g], gc(g)['nat']) for c, g in CG}
        bd_p2 = {(c, g): bdiag(p2[c, g], gc(g)['nat']) for c, g in CG}
        rq = {p: rg[p] - mm(mra_m[p], bd_p1[p]) for p in CG}
        yc = {p: mrkv[p] - mm(mra_m[p], bd_p2[p]) for p in CG}

        yield
        def head_cat(z, nh):
            return jnp.concatenate([z[i * HEAD_DIM:(i + 1) * HEAD_DIM, :] for i in range(nh)], axis=1)

        def glanes(g):
            h0, nh = GROUPS[g]
            return slice(h0 * HEAD_DIM, (h0 + nh) * HEAD_DIM)

        sq64 = {nh: jnp.where((_iota2((nh * HEAD_DIM, nh * HEAD_DIM), 0) >> 6)
                              == (_iota2((nh * HEAD_DIM, nh * HEAD_DIM), 1) >> 6), 1.0, 0.0).astype(BF16)
                for _, nh in GROUPS}
        at_cat = {(c, g): head_cat(rk_s[6, rws[c], glanes(g)].T, GROUPS[g][1]) for c, g in CG}
        kt_cat = {(c, g): head_cat(rk_s[5, rws[c], glanes(g)].T, GROUPS[g][1]) for c, g in CG}
        wc_cat = {(c, g): head_cat(jnp.broadcast_to(w_end[c][:, glanes(g)],
                                                    (HEAD_DIM, GROUPS[g][1] * HEAD_DIM)).T, GROUPS[g][1])
                  for c, g in CG}
        yield
        g_cat = {p: mm(at_cat[p], bd_p1[p]) for p in CG}
        h_cat = {p: mm(kt_cat[p], bd_v[p]) - mm(at_cat[p], bd_p2[p]) for p in CG}
        yield
        NG = range(len(GROUPS))
        s_cat = [wkv_s[:, glanes(g)] for g in NG]
        for c in CH:
            for g in NG:
                nh = GROUPS[g][1]
                bd_s = jnp.concatenate([s_cat[g].astype(BF16)] * nh, axis=0) * sq64[nh]
                ys = mm(jnp.concatenate([rq[c, g], g_cat[c, g]], axis=0), bd_s)
                yb_s[rws[c], glanes(g)] = ys[0:L] + yc[c, g]
                s_cat[g] = wc_cat[c, g] * s_cat[g] - ys[L:L + HEAD_DIM] + h_cat[c, g]
            yield
        for g in NG:
            h0, nh = GROUPS[g]
            wkv_s[:, glanes(g)] = s_cat[g]
            s_tr = s_cat[g].T
            for i in range(nh):
                wkv_o[0, h0 + i] = s_tr[i * HEAD_DIM:(i + 1) * HEAD_DIM, :]

        y_b = yb_s[...]
        y_cen = y_b - segsum(y_b) * (1.0 / HEAD_DIM)
        yield
        y_var = segsum(y_cen * y_cen) * (1.0 / HEAD_DIM)
        y_n = y_cen * lax.rsqrt(y_var + GN_EPS_B) * vec(V_LNW, D_B) + vec(V_LNB, D_B)
        yb_out = (y_n + bonus) * g_b
        yield
        ybc_s[:, 0:D_B] = yb_out

    def group_c():
        xconv = taps(OFF_PC, D_C, V_CONV_C_W) + vec(V_CONV_C_B, D_C)
        xact = xconv * _sigmoid(xconv)
        qk = _bdot(xact, wqk_ref[0])
        q_c = qk[:, 0:D_C]
        k_c = qk[:, D_C:2 * D_C]
        vc = proj_s[HIST:HIST + TT, OFF_PC + D_C:OFF_PC + 2 * D_C]
        gates_c = (_bdot(q_c, wif_ref[0, 0:D_C, :]) + _bdot(k_c, wif_ref[0, D_C:2 * D_C, :])
                   + _bdot(vc, wif_ref[0, 2 * D_C:3 * D_C, :]))
        i_pre = gates_c[:, 0:GATE_W] + vec(V_BIF, GATE_W)
        logf = -_softplus(-(gates_c[:, GATE_W:2 * GATE_W] + vec(V_BIF2, GATE_W)))
        ml_s[0] = q_c
        ml_s[1] = k_c * (HEAD_DIM ** -0.5)

        yield
        b_col = _edot_l(blk_tril, logf)
        yield
        g_col = i_pre - b_col
        g_max = g_col
        pos = row_t & (L - 1)
        d = 1
        while d < L:
            g_max = jnp.where(pos >= d, jnp.maximum(g_max, pltpu.roll(g_max, d, 0)), g_max)
            d *= 2
        yield
        m_loc = b_col + g_max
        g_last = [g_max[c * L + L - 1:c * L + L, :] for c in CH]
        b_end = [b_col[c * L + L - 1:c * L + L, :] for c in CH]
        m_loc_end = [b_end[c] + g_last[c] for c in CH]
        w_end_loc = jnp.concatenate([jnp.exp(g_col[rws[c], :] - g_last[c]) for c in CH], axis=0)
        kw = ml_s[1] * expand(w_end_loc, n=1)
        yield
        s_of_lane = _iota2((L, N_HEAD_C * L), 1) & (L - 1)
        t_of_row = _iota2((L, N_HEAD_C * L), 0)
        causal_t = t_of_row >= s_of_lane
        eye_t = t_of_row == s_of_lane
        bd_l = (_iota2((N_HEAD_C * L, D_C), 0) >> LOG_L) == (_iota2((N_HEAD_C * L, D_C), 1) >> 6)
        seg_ones_l = jnp.where((_iota2((N_HEAD_C * L, GATE_W), 0) >> LOG_L)
                               == _iota2((N_HEAD_C * L, GATE_W), 1), 1.0, 0.0).astype(BF16)
        expand_l = jnp.where(_iota2((GATE_W, N_HEAD_C * L), 0)
                             == (_iota2((GATE_W, N_HEAD_C * L), 1) >> LOG_L), 1.0, 0.0).astype(BF16)
        g_exp_l = _edot_r(g_col, expand_l, n=2)
        g_max_exp_l = _edot_r(g_max, expand_l, n=2)

        pws = [slice(p * PAIR_W, (p + 1) * PAIR_W) for p in range(N_PAIR_C)]
        pair_mask = (_iota2((PAIR_W, PAIR_W), 0) >> 6) == (_iota2((PAIR_W, PAIR_W), 1) >> 6)
        yield
        p_loc, den_loc, c_loc, n_loc = [], [], [], []
        for c in CH:
            q_cc = ml_s[0, rws[c], :]
            k_cc = ml_s[1, rws[c], :]
            v_cc = proj_s[HIST + c * L:HIST + (c + 1) * L, OFF_PC + D_C:OFF_PC + 2 * D_C]
            k_bd = jnp.where(bd_l, jnp.concatenate([k_cc] * N_HEAD_C, axis=0), 0.0)
            v_bd = jnp.where(bd_l, jnp.concatenate([v_cc] * N_HEAD_C, axis=0), 0.0)
            g_flat = jnp.sum(jnp.where(eye_t, g_exp_l[rws[c], :], 0.0), axis=0, keepdims=True)
            dmat = jnp.where(causal_t, jnp.exp(jnp.minimum(g_flat - g_max_exp_l[rws[c], :], 0.0)), 0.0)
            scores = _bdot_nt(q_cc, k_bd) * dmat
            p_loc.append(_bdot(scores, v_bd))
            den_loc.append(_edot_r(scores, seg_ones_l, n=1))
            kw_c = kw[rws[c], :]
            kw_t = kw_c.T
            c_loc.append([jnp.where(pair_mask, _bdot(kw_t[pws[p], :], v_cc[:, pws[p]]), 0.0)
                          for p in range(N_PAIR_C)])
            n_loc.append(jnp.sum(kw_c, axis=0, keepdims=True))
            yield

        m_in = []
        m_cur = mrow_s[...]
        for c in CH:
            m_in.append(m_cur)
            m_cur = jnp.maximum(b_end[c] + m_cur, m_loc_end[c])
        m_out = m_in[1:] + [m_cur]
        mrow_s[...] = m_cur
        scale_rows = jnp.concatenate(
            [jnp.exp(b_end[c] + m_in[c] - m_out[c]) for c in CH]
            + [jnp.exp(m_loc_end[c] - m_out[c]) for c in CH], axis=0)
        scale_exp = expand(scale_rows)
        yield
        c_in, n_in = [], []
        c_cur = [cbd_s[p] for p in range(N_PAIR_C)]
        n_cur = nrow_s[...]
        for c in CH:
            c_in.append(c_cur)
            n_in.append(n_cur)
            w0 = scale_exp[c:c + 1, :]
            wl = scale_exp[NC + c:NC + c + 1, :]
            c_cur = [c_cur[p] * w0[:, pws[p]] + c_loc[c][p] * wl[:, pws[p]] for p in range(N_PAIR_C)]
            n_cur = n_cur * w0 + n_loc[c] * wl
        for p in range(N_PAIR_C):
            cbd_s[p] = c_cur[p]
        nrow_s[...] = n_cur
        for h in range(N_HEAD_C):
            mc_o[0, h] = c_cur[h // 2][sls[h % 2], sls[h % 2]]
            mn_o[0, h:h + 1, :] = n_cur[:, sls[h]]
        mm_o[0] = m_cur[:, 0:N_HEAD_C]

        yield
        m0_b = jnp.concatenate([jnp.broadcast_to(m_in[c], (L, GATE_W)) for c in CH], axis=0)
        log_inter = b_col + m0_b
        m_t = jnp.maximum(log_inter, m_loc)
        w_inter = jnp.exp(log_inter - m_t)
        w_local = jnp.exp(m_loc - m_t)
        q_all = ml_s[0]
        qc0 = jnp.concatenate(
            [jnp.concatenate([_bdot(q_all[rws[c], pws[p]], c_in[c][p]) for p in range(N_PAIR_C)], axis=1)
             for c in CH], axis=0)
        qn0 = jnp.concatenate([_edot_r(q_all[rws[c], :] * n_in[c], seg_ones, n=1) for c in CH], axis=0)
        yield
        den = w_inter * qn0 + w_local * jnp.concatenate(den_loc, axis=0)
        inv_den = 1.0 / jnp.maximum(jnp.abs(den), jnp.exp(-m_t))
        h_c = (expand(w_inter * inv_den) * qc0
               + expand(w_local * inv_den) * jnp.concatenate(p_loc, axis=0))

        h_cen = h_c - segsum(h_c) * (1.0 / HEAD_DIM)
        yield
        h_var = segsum(h_cen * h_cen) * (1.0 / HEAD_DIM)
        zc = proj_s[HIST:HIST + TT, OFF_PC + 2 * D_C:OFF_PC + 3 * D_C]
        yc_out = _sigmoid(zc) * (h_cen * lax.rsqrt(h_var + GN_EPS_C) * vec(V_GN, D_C))
        yield
        ybc_s[:, D_B:D_B + D_C] = yc_out

    pending = [group_a(), group_b(), group_c()]
    while pending:
        pending = [g for g in pending if next(g, StopIteration) is not StopIteration]
        yield

    last3 = proj_s[HIST + TT - 3:HIST + TT, :]
    proj_s[HIST - 3:HIST, :] = last3
    ca_o[0] = last3[:, 0:D_A]
    sh_o[0] = last3[2:3, OFF_PB:OFF_PB + D_B_IN]
    cc_o[0] = last3[:, OFF_PC:OFF_PC + D_C]

    xo_ref[0] = (x_ref[0] + out_parts['a']) + jnp.dot(ybc_s[...].astype(BF16), w_out_ref[0, D_A:D_MODEL, :],
                                                      preferred_element_type=F32)


def _mixer_call(layer, x, states, wts, TT, L, G, ffn_final=None):
    B, T, _ = x.shape
    NT = T // TT
    assert B % G == 0
    single = None if ffn_final is None else pl.Buffered(1)

    def state_in_spec(shape):
        nd = len(shape)
        return pl.BlockSpec((None, G) + tuple(shape[2:]), lambda b, j: (layer, b) + (0,) * (nd - 2))

    def state_out_spec(shape):
        nd = len(shape)
        return pl.BlockSpec((G,) + tuple(shape[1:]), lambda b, j: (b,) + (0,) * (nd - 1))

    def layer_spec(arr):
        nd = arr.ndim
        return pl.BlockSpec((1,) + tuple(arr.shape[1:]), lambda b, j: (layer,) + (0,) * (nd - 1),
                            pipeline_mode=single)

    x_spec = pl.BlockSpec((G, TT, D_MODEL), lambda b, j: (b, j, 0))
    state_in_specs = [state_in_spec(s.shape) for s in states]
    state_out_specs = [state_out_spec(s.shape[1:]) for s in states]
    w_names = ('vec', 'w_in', 'lruw', 'w2', 'a2', 'g2', 'wqk', 'wif', 'w_out')
    if ffn_final is not None:
        w_names += ('ffn_vec', 'w_ffn_in', 'w_ffn_out')
    w_arrs = [wts[n] for n in w_names]

    out_shape = ([jax.ShapeDtypeStruct(x.shape, F32)]
                 + [jax.ShapeDtypeStruct(s.shape[1:], F32) for s in states])
    scratch = [
        pltpu.VMEM((G, HIST + TT, D_IN), F32),
        pltpu.VMEM((G, 7, TT, D_B), F32),
        pltpu.VMEM((G, 2, TT, D_C), F32),
        pltpu.VMEM((G, TT, D_B), F32),
        pltpu.VMEM((G, TT, D_B + D_C), F32),
        pltpu.VMEM((G, HEAD_DIM, D_B), F32),
        pltpu.VMEM((G, N_PAIR_C, PAIR_W, PAIR_W), F32),
        pltpu.VMEM((G, 1, D_C), F32),
        pltpu.VMEM((G, 1, GATE_W), F32),
    ]
    outs = pl.pallas_call(
        functools.partial(_mixer_kernel, TT, L, G, ffn_final),
        grid=(B // G, NT),
        in_specs=[x_spec] + state_in_specs + [layer_spec(a) for a in w_arrs],
        out_specs=[x_spec] + state_out_specs,
        out_shape=out_shape,
        scratch_shapes=scratch,
        compiler_params=pltpu.CompilerParams(
            dimension_semantics=("arbitrary", "arbitrary"),
            vmem_limit_bytes=VMEM_LIMIT_MIXER),
        name="mixer",
    )(x, *states, *w_arrs)
    return outs[0], tuple(outs[1:])


FF_CHUNK = 768


def _ffn_kernel(final, x_ref, vec_ref, wi_ref, wo_ref, o_ref):
    o_ref[...] = _ffn_rows(final, x_ref[...], vec_ref, wi_ref, wo_ref)


def _ffn_rows(final, x, vec_ref, wi_ref, wo_ref):
    hn = _rms(x, vec_ref[0, 0:1, :]).astype(BF16)
    acc = x
    for lo in range(0, D_FF, FF_CHUNK):
        hi = min(lo + FF_CHUNK, D_FF)
        gate = jnp.dot(hn, wi_ref[0, :, lo:hi], preferred_element_type=F32)
        up = jnp.dot(hn, wi_ref[0, :, D_FF + lo:D_FF + hi], preferred_element_type=F32)
        act = (gate * _sigmoid(gate) * up).astype(BF16)
        acc = acc + jnp.dot(act, wo_ref[0, lo:hi, :], preferred_element_type=F32)
    if final:
        acc = _rms(acc, vec_ref[0, 1:2, :])
    return acc


def _ffn_call(layer, x2d, wts, TM, final):
    M = x2d.shape[0]
    return pl.pallas_call(
        functools.partial(_ffn_kernel, final),
        grid=(M // TM,),
        in_specs=[
            pl.BlockSpec((TM, D_MODEL), lambda i: (i, 0)),
            pl.BlockSpec((1, 8, D_MODEL), lambda i: (layer, 0, 0)),
            pl.BlockSpec((1, D_MODEL, 2 * D_FF), lambda i: (layer, 0, 0),
                         pipeline_mode=pl.Buffered(1)),
            pl.BlockSpec((1, D_FF, D_MODEL), lambda i: (layer, 0, 0),
                         pipeline_mode=pl.Buffered(1)),
        ],
        out_specs=pl.BlockSpec((TM, D_MODEL), lambda i: (i, 0)),
        out_shape=jax.ShapeDtypeStruct(x2d.shape, F32),
        compiler_params=pltpu.CompilerParams(
            dimension_semantics=("arbitrary",),
            vmem_limit_bytes=VMEM_LIMIT_FFN),
        name="ffn",
    )(x2d, wts['ffn_vec'], wts['w_ffn_in'], wts['w_ffn_out'])


def _block_diag(w):
    P, n, d, e = w.shape
    eye = jnp.eye(n, dtype=w.dtype)
    return (w[:, :, :, None, :] * eye[None, :, None, :, None]).reshape(P, n * d, n * e)


def _pack_rows(P, rows):
    parts = []
    for r, arr in rows:
        arr = arr.reshape(P, -1, arr.shape[-1]).astype(F32)
        assert r == sum(p.shape[1] for p in parts)
        parts.append(jnp.pad(arr, ((0, 0), (0, 0), (0, VEC_W - arr.shape[2]))))
    used = sum(p.shape[1] for p in parts)
    parts.append(jnp.zeros((P, N_VEC - used, VEC_W), F32))
    return jnp.concatenate(parts, axis=1)


def _run_stream(x, states, wts, TT, L, TM, G):
    B, T, _ = x.shape
    new_states = []
    for l in range(DEPTH):
        last = l == DEPTH - 1
        fuse_ffn = G == B and T == TT
        x, st_new = _mixer_call(l, x, states, wts, TT, L, G, last if fuse_ffn else None)
        new_states.append(st_new)
        if not fuse_ffn:
            x = _ffn_call(l, x.reshape(B * T, D_MODEL), wts, TM, last).reshape(B, T, D_MODEL)
    stacked = tuple(jnp.stack([st[i] for st in new_states]) for i in range(len(states)))
    return x, stacked


def kernel(x_prompt, x_sample, state_conv_a, state_lru, state_shift_b, state_wkv, state_conv_c, state_mem_c, state_mem_n, state_mem_m, norm1, w_in, conv_a_w, conv_a_b, lru_wr, lru_br, lru_wi, lru_bi, lru_lambda, norm_a, rwkv_mu, rwkv_w0, rwkv_w2, rwkv_a0, rwkv_a2, rwkv_g2, rwkv_kk, rwkv_ka, rwkv_rk, rwkv_lnw, rwkv_lnb, conv_c_w, conv_c_b, mlstm_wq, mlstm_wk, mlstm_wif, mlstm_bif, mlstm_gn, w_out, norm2, w_ffn_in, w_ffn_out, norm_f):
    P = DEPTH
    gate_pad = jnp.zeros((P, 3 * D_C, GATE_W - N_HEAD_C), F32)
    wif_pad = jnp.concatenate([mlstm_wif[:, :, 0:N_HEAD_C], gate_pad,
                               mlstm_wif[:, :, N_HEAD_C:], gate_pad], axis=-1)
    wts = {
        'vec': _pack_rows(P, [
            (V_NORM1, norm1), (V_CONV_A_B, conv_a_b),
            (V_LRU_B, jnp.concatenate([lru_br, lru_bi], axis=-1)), (V_LAM, lru_lambda),
            (V_NORM_A, norm_a), (V_MU, rwkv_mu), (V_W0, rwkv_w0), (V_A0, rwkv_a0),
            (V_KK, rwkv_kk), (V_KA, rwkv_ka), (V_RK, rwkv_rk.reshape(P, D_B)),
            (V_LNW, rwkv_lnw), (V_LNB, rwkv_lnb), (V_CONV_C_B, conv_c_b),
            (V_BIF, mlstm_bif[:, 0:N_HEAD_C]), (V_GN, mlstm_gn),
            (V_CONV_A_W, conv_a_w), (V_CONV_C_W, conv_c_w), (V_BIF2, mlstm_bif[:, N_HEAD_C:])]),
        'w_in': w_in.astype(BF16),
        'lruw': jnp.concatenate([_block_diag(lru_wr), _block_diag(lru_wi)], axis=-1).astype(BF16),
        'w2': rwkv_w2.astype(BF16),
        'a2': rwkv_a2.astype(BF16),
        'g2': rwkv_g2.astype(BF16),
        'wqk': jnp.concatenate([_block_diag(mlstm_wq), _block_diag(mlstm_wk)], axis=-1).astype(BF16),
        'wif': wif_pad.astype(BF16),
        'w_out': w_out.astype(BF16),
        'ffn_vec': jnp.concatenate([norm2[:, None, :],
                                    jnp.broadcast_to(norm_f[None, None, :], (P, 1, D_MODEL)),
                                    jnp.zeros((P, 6, D_MODEL), F32)], axis=1),
        'w_ffn_in': w_ffn_in.astype(BF16),
        'w_ffn_out': w_ffn_out.astype(BF16),
    }

    def to_kernel_layout(conv_a, lru, shift, wkv, conv_c, mem_c, mem_n, mem_m):
        Bn = lru.shape[1]
        return (conv_a, lru.reshape(P, Bn, 1, D_A), shift, wkv, conv_c,
                mem_c, mem_n, mem_m.reshape(P, Bn, 1, N_HEAD_C))

    def from_kernel_layout(conv_a, lru, shift, wkv, conv_c, mem_c, mem_n, mem_m):
        Bn = lru.shape[1]
        return (conv_a, lru.reshape(P, Bn, D_A), shift, wkv, conv_c,
                mem_c, mem_n, mem_m.reshape(P, Bn, N_HEAD_C))

    Bp, Tp, _ = x_prompt.shape
    Bs, Ts, _ = x_sample.shape
    zero_states = (jnp.zeros((P, Bp, CONV_W - 1, D_A), F32),
                   jnp.zeros((P, Bp, D_A), F32),
                   jnp.zeros((P, Bp, 1, D_B_IN), F32),
                   jnp.zeros((P, Bp, N_HEAD_B, HEAD_DIM, HEAD_DIM), F32),
                   jnp.zeros((P, Bp, CONV_W - 1, D_C), F32),
                   jnp.zeros((P, Bp, N_HEAD_C, HEAD_DIM, HEAD_DIM), F32),
                   jnp.zeros((P, Bp, N_HEAD_C, HEAD_DIM), F32),
                   jnp.zeros((P, Bp, N_HEAD_C), F32))
    Lp = MLSTM_CHUNK if Tp % MLSTM_CHUNK == 0 else Tp
    Ls = MLSTM_CHUNK if Ts % MLSTM_CHUNK == 0 else Ts
    TTp = 256 if Tp % 256 == 0 else Lp
    y_prompt, p_states = _run_stream(x_prompt, to_kernel_layout(*zero_states), wts, TTp, Lp, 512, G_PROMPT)
    y_sample, s_states = _run_stream(
        x_sample, to_kernel_layout(state_conv_a, state_lru, state_shift_b, state_wkv,
                                   state_conv_c, state_mem_c, state_mem_n, state_mem_m),
        wts, Ls, Ls, Bs * Ts, G_SAMPLE)
    return (y_prompt, y_sample) + from_kernel_layout(*p_states) + from_kernel_layout(*s_states)
```
